```python
import math
import jax, jax.numpy as jnp
from jax import lax
import numpy as np

D_MODEL = 1024
BATCH = 16
SEQ = 2048
DEPTH = 2

N_EVEN = (DEPTH + 1) // 2
N_ODD = DEPTH // 2

MLA_HEADS = 8
Q_LORA = 256
KV_LORA = 128
QK_NOPE = 64
QK_ROPE = 32
V_HEAD = 64
MLA_WIDTH = MLA_HEADS * V_HEAD
ROPE_BASE = 10000.0
Q_BLOCK = 128

LRU_HEADS = 8
LRU_WIDTH = 512
LRU_BLOCK = LRU_WIDTH // LRU_HEADS
LRU_CONV = 4
LRU_C = 8.0

AB_IN = Q_LORA + KV_LORA + QK_ROPE + 2 * LRU_WIDTH
AB_MIX = MLA_WIDTH + LRU_WIDTH

CHUNK = 128
SGU_GROUPS = 8
SGU_WIDTH = D_MODEL
SGU_GROUP_DIM = SGU_WIDTH // SGU_GROUPS

D_FF = 2816
FFN_CONV = 3

NORM_EPS = 1e-6

kernel_name = "hybrid_mla_rglru_chunksgu_convffn"


def rms_norm(x, g):
    xf = x.astype(jnp.float32)
    y = xf * lax.rsqrt(jnp.mean(xf * xf, axis=-1, keepdims=True) + NORM_EPS)
    return (y * g.astype(jnp.float32)).astype(x.dtype)


def layer_norm(x, g, b):
    xf = x.astype(jnp.float32)
    mu = jnp.mean(xf, axis=-1, keepdims=True)
    xc = xf - mu
    y = xc * lax.rsqrt(jnp.mean(xc * xc, axis=-1, keepdims=True) + NORM_EPS)
    return (y * g.astype(jnp.float32) + b.astype(jnp.float32)).astype(x.dtype)


def causal_dwconv(x, w, b):
    K = w.shape[0]
    S = x.shape[1]
    xp = jnp.pad(x, ((0, 0), (K - 1, 0), (0, 0)))
    y = xp[:, 0:S] * w[0]
    for k in range(1, K):
        y = y + xp[:, k:k + S] * w[k]
    return y + b


def rope(x, positions):
    half = x.shape[-1] // 2
    inv_freq = jnp.exp(-math.log(ROPE_BASE) * jnp.arange(half, dtype=jnp.float32) / half)
    ang = positions.astype(jnp.float32)[..., None] * inv_freq
    cos = jnp.cos(ang)[:, :, None, :]
    sin = jnp.sin(ang)[:, :, None, :]
    xf = x.astype(jnp.float32)
    x1, x2 = xf[..., :half], xf[..., half:]
    return jnp.concatenate([x1 * cos - x2 * sin, x2 * cos + x1 * sin], axis=-1).astype(x.dtype)


def causal_block_attention(q, k, v):
    B, S, H, Dk = q.shape
    Dv = v.shape[-1]
    nb = S // Q_BLOCK
    scale = Dk ** -0.5
    qb = q.reshape(B, nb, Q_BLOCK, H, Dk).transpose(1, 0, 2, 3, 4)
    kpos = jnp.arange(S)

    def one_block(args):
        q_blk, blk = args
        s = jnp.einsum('bqhd,bkhd->bhqk', q_blk, k,
                       preferred_element_type=jnp.float32) * scale
        qpos = blk * Q_BLOCK + jnp.arange(Q_BLOCK)
        s = jnp.where(kpos[None, :] <= qpos[:, None], s, -jnp.inf)
        p = jax.nn.softmax(s, axis=-1)
        return jnp.einsum('bhqk,bkhd->bqhd', p.astype(v.dtype), v)

    o = lax.map(one_block, (qb, jnp.arange(nb)))
    return o.transpose(1, 0, 2, 3, 4).reshape(B, S, H * Dv)


def rg_lru(x, w_a, b_a, w_x, b_x, lam):
    B, S, C = x.shape
    xg = x.reshape(B, S, LRU_HEADS, LRU_BLOCK)
    r = jax.nn.sigmoid(jnp.einsum('bsgi,gij->bsgj', xg, w_a).reshape(B, S, C) + b_a).astype(jnp.float32)
    i = jax.nn.sigmoid(jnp.einsum('bsgi,gij->bsgj', xg, w_x).reshape(B, S, C) + b_x).astype(jnp.float32)
    log_a = -LRU_C * r * jax.nn.softplus(-lam.astype(jnp.float32))
    a = jnp.exp(log_a)
    bx = jnp.sqrt(-jnp.expm1(2.0 * log_a)) * (i * x.astype(jnp.float32))

    def combine(left, right):
        a_l, b_l = left
        a_r, b_r = right
        return a_l * a_r, a_r * b_l + b_r

    _, h = lax.associative_scan(combine, (a, bx), axis=1)
    return h.astype(x.dtype)


def mla_lru_mixer(h, positions, w_in, q_norm, w_q_b, kv_norm, w_kv_b, conv_w, conv_b,
                  w_rg_a, b_rg_a, w_rg_x, b_rg_x, lam, w_out):
    B, S, _ = h.shape
    z = h @ w_in
    o1 = Q_LORA
    o2 = o1 + KV_LORA
    o3 = o2 + QK_ROPE
    o4 = o3 + LRU_WIDTH
    c_q, c_kv, k_pe, x_lru, gate_lru = jnp.split(z, [o1, o2, o3, o4], axis=-1)

    q = (rms_norm(c_q, q_norm) @ w_q_b).reshape(B, S, MLA_HEADS, QK_NOPE + QK_ROPE)
    q = jnp.concatenate([q[..., :QK_NOPE], rope(q[..., QK_NOPE:], positions)], axis=-1)
    kv = (rms_norm(c_kv, kv_norm) @ w_kv_b).reshape(B, S, MLA_HEADS, QK_NOPE + V_HEAD)
    k_pe = jnp.broadcast_to(rope(k_pe[:, :, None, :], positions), (B, S, MLA_HEADS, QK_ROPE))
    k = jnp.concatenate([kv[..., :QK_NOPE], k_pe], axis=-1)
    v = kv[..., QK_NOPE:]
    y_mla = causal_block_attention(q, k, v)

    xc = causal_dwconv(x_lru, conv_w, conv_b)
    y_lru = rg_lru(xc, w_rg_a, b_rg_a, w_rg_x, b_rg_x, lam) * jax.nn.gelu(gate_lru)

    return jnp.concatenate([y_mla, y_lru], axis=-1) @ w_out


def chunk_sgu_mixer(h, w_in, ln_g, ln_b, w_s, b_s, w_out):
    B, S, _ = h.shape
    z = jax.nn.gelu(h @ w_in)
    u, v = jnp.split(z, 2, axis=-1)
    v = layer_norm(v, ln_g, ln_b).reshape(B, S // CHUNK, CHUNK, SGU_GROUPS, SGU_GROUP_DIM)
    causal = jnp.tril(jnp.ones((CHUNK, CHUNK), dtype=w_s.dtype))
    s = jnp.einsum('gts,bnsgc->bntgc', w_s * causal, v) + b_s.T[:, :, None]
    return (u * s.reshape(B, S, SGU_WIDTH)) @ w_out


def conv_ffn(h, w_gate, w_up, conv_w, conv_b, w_down):
    g = causal_dwconv(h @ w_gate, conv_w, conv_b)
    return (jax.nn.gelu(g) * (h @ w_up)) @ w_down


def setup_inputs(seed: int = 0) -> dict:
    key = jax.random.key(seed)
    ks = jax.random.split(key, 32)
    f32 = jnp.float32

    def nrm(k, shape, fan_in):
        return jax.random.normal(k, shape, f32) * (fan_in ** -0.5)

    def gain(k, shape, s=0.02):
        return 1.0 + s * jax.random.normal(k, shape, f32)

    def bias(k, shape):
        return 0.02 * jax.random.normal(k, shape, f32)

    x = jax.random.normal(ks[0], (BATCH, SEQ, D_MODEL), f32)
    positions = jnp.broadcast_to(jnp.arange(SEQ, dtype=jnp.int32), (BATCH, SEQ))

    a0 = jax.random.uniform(ks[13], (N_EVEN, LRU_WIDTH), f32, minval=0.9, maxval=0.999)
    lam = jnp.log(a0) - jnp.log1p(-a0)

    return {
        "x": x,
        "positions": positions,
        "ab_norm": gain(ks[1], (N_EVEN, D_MODEL)),
        "ab_w_in": nrm(ks[2], (N_EVEN, D_MODEL, AB_IN), D_MODEL),
        "ab_q_norm": gain(ks[3], (N_EVEN, Q_LORA)),
        "ab_w_q_b": nrm(ks[4], (N_EVEN, Q_LORA, MLA_HEADS * (QK_NOPE + QK_ROPE)), Q_LORA),
        "ab_kv_norm": gain(ks[5], (N_EVEN, KV_LORA)),
        "ab_w_kv_b": nrm(ks[6], (N_EVEN, KV_LORA, MLA_HEADS * (QK_NOPE + V_HEAD)), KV_LORA),
        "ab_conv_w": nrm(ks[7], (N_EVEN, LRU_CONV, LRU_WIDTH), LRU_CONV),
        "ab_conv_b": bias(ks[8], (N_EVEN, LRU_WIDTH)),
        "ab_w_rg_a": nrm(ks[9], (N_EVEN, LRU_HEADS, LRU_BLOCK, LRU_BLOCK), LRU_BLOCK),
        "ab_b_rg_a": bias(ks[10], (N_EVEN, LRU_WIDTH)),
        "ab_w_rg_x": nrm(ks[11], (N_EVEN, LRU_HEADS, LRU_BLOCK, LRU_BLOCK), LRU_BLOCK),
        "ab_b_rg_x": bias(ks[12], (N_EVEN, LRU_WIDTH)),
        "ab_lambda": lam,
        "ab_w_out": nrm(ks[14], (N_EVEN, AB_MIX, D_MODEL), AB_MIX),
        "c_norm": gain(ks[15], (N_ODD, D_MODEL)),
        "c_w_in": nrm(ks[16], (N_ODD, D_MODEL, 2 * SGU_WIDTH), D_MODEL),
        "c_ln_g": gain(ks[17], (N_ODD, SGU_WIDTH)),
        "c_ln_b": bias(ks[18], (N_ODD, SGU_WIDTH)),
        "c_w_s": nrm(ks[19], (N_ODD, SGU_GROUPS, CHUNK, CHUNK), CHUNK),
        "c_b_s": gain(ks[20], (N_ODD, SGU_GROUPS, CHUNK), 0.1),
        "c_w_out": nrm(ks[21], (N_ODD, SGU_WIDTH, D_MODEL), SGU_WIDTH),
        "ffn_norm": gain(ks[22], (DEPTH, D_MODEL)),
        "ffn_w_gate": nrm(ks[23], (DEPTH, D_MODEL, D_FF), D_MODEL),
        "ffn_w_up": nrm(ks[24], (DEPTH, D_MODEL, D_FF), D_MODEL),
        "ffn_conv_w": nrm(ks[25], (DEPTH, FFN_CONV, D_FF), FFN_CONV),
        "ffn_conv_b": bias(ks[26], (DEPTH, D_FF)),
        "ffn_w_down": nrm(ks[27], (DEPTH, D_FF, D_MODEL), D_FF),
        "final_norm": gain(ks[28], (D_MODEL,)),
    }


def reference(x, positions, ab_norm, ab_w_in, ab_q_norm, ab_w_q_b, ab_kv_norm, ab_w_kv_b,
              ab_conv_w, ab_conv_b, ab_w_rg_a, ab_b_rg_a, ab_w_rg_x, ab_b_rg_x, ab_lambda,
              ab_w_out, c_norm, c_w_in, c_ln_g, c_ln_b, c_w_s, c_b_s, c_w_out,
              ffn_norm, ffn_w_gate, ffn_w_up, ffn_conv_w, ffn_conv_b, ffn_w_down, final_norm):
    h = x
    for layer in range(DEPTH):
        if layer % 2 == 0:
            i = layer // 2
            h = h + mla_lru_mixer(rms_norm(h, ab_norm[i]), positions, ab_w_in[i],
                                  ab_q_norm[i], ab_w_q_b[i], ab_kv_norm[i], ab_w_kv_b[i],
                                  ab_conv_w[i], ab_conv_b[i], ab_w_rg_a[i], ab_b_rg_a[i],
                                  ab_w_rg_x[i], ab_b_rg_x[i], ab_lambda[i], ab_w_out[i])
        else:
            i = layer // 2
            h = h + chunk_sgu_mixer(rms_norm(h, c_norm[i]), c_w_in[i], c_ln_g[i], c_ln_b[i],
                                    c_w_s[i], c_b_s[i], c_w_out[i])
        h = h + conv_ffn(rms_norm(h, ffn_norm[layer]), ffn_w_gate[layer], ffn_w_up[layer],
                         ffn_conv_w[layer], ffn_conv_b[layer], ffn_w_down[layer])
    return rms_norm(h, final_norm)
```

```python
import functools
import math

import jax
import jax.numpy as jnp
from jax import lax
from jax.experimental import pallas as pl
from jax.experimental.pallas import tpu as pltpu

F32 = jnp.float32
BF16 = jnp.bfloat16

D_MODEL = 1024
MLA_HEADS = 8
Q_LORA = 256
KV_LORA = 128
QK_NOPE = 64
QK_ROPE = 32
V_HEAD = 64
ROPE_BASE = 10000.0
LRU_WIDTH = 512
LRU_HEADS = 8
LRU_CONV = 4
LRU_C = 8.0
CHUNK = 128
SGU_GROUPS = 8
D_FF = 2816
FFN_CONV = 3
NORM_EPS = 1e-6

LANES = 128
SUBLANES = 8
HEAD_PAD = 128
AB_IN_PAD = Q_LORA + KV_LORA + HEAD_PAD + 2 * LRU_WIDTH

ROW_TILE = 512
ATTN_TILE = 256
LRU_TILE = 256
FF_CHUNK = 256
N_FF_CHUNKS = D_FF // FF_CHUNK
VMEM_LIMIT = 56 * 1024 * 1024


def _dot(a, b):
    return jnp.dot(a, b, preferred_element_type=F32)


def _dot_nt(a, b):
    return lax.dot_general(a, b, (((1,), (1,)), ((), ())), preferred_element_type=F32)


def _gelu(x):
    c = math.sqrt(2.0 / math.pi)
    return 0.5 * x * (1.0 + jnp.tanh(c * (x + 0.044715 * (x * x * x))))


def _expm1(x):
    u = jnp.exp(x)
    small = jnp.where(u == 1.0, x, (u - 1.0) * x / jnp.log(u))
    return jnp.where(x < -0.5, u - 1.0, small)


def _rms(x, g):
    ms = jnp.mean(x * x, axis=-1, keepdims=True)
    return x * lax.rsqrt(ms + NORM_EPS) * g


def _shift_rows(x, prev8, k):
    rolled = pltpu.roll(x, k, 0)
    row = lax.broadcasted_iota(jnp.int32, (SUBLANES, x.shape[1]), 0)
    top = jnp.where(row < k, pltpu.roll(prev8, k, 0), rolled[:SUBLANES])
    return jnp.concatenate([top, rolled[SUBLANES:]], axis=0)


def _const_spec(shape):
    nd = len(shape)
    return pl.BlockSpec(shape, lambda *_: (0,) * nd, pipeline_mode=pl.Buffered(1))


def _ab_in_kernel(h_ref, pos_ref, g_ref, w_in_ref, qg_ref, wq_ref, kvg_ref, wk_ref, wv_ref,
                  invf_ref, q_ref, k_ref, v_ref, xl_ref, gate_ref, *, scale):
    tm = h_ref.shape[0]
    xn = _rms(h_ref[...], g_ref[...]).astype(BF16)

    o1 = Q_LORA
    o2 = o1 + KV_LORA
    o3 = o2 + HEAD_PAD
    o4 = o3 + LRU_WIDTH
    xl_ref[...] = _dot(xn, w_in_ref[:, o3:o4])
    gate_ref[...] = _dot(xn, w_in_ref[:, o4:])
    c_q = _dot(xn, w_in_ref[:, :o1])
    c_kv = _dot(xn, w_in_ref[:, o1:o2])
    kpe = _dot(xn, w_in_ref[:, o2:o3])

    ang = pos_ref[...].astype(F32) * invf_ref[...]
    lane = lax.broadcasted_iota(jnp.int32, (tm, LANES), 1)
    cosv = jnp.cos(ang)
    sinv = jnp.sin(ang)
    x1_lo, x2_lo, x2_hi = QK_NOPE, QK_NOPE + QK_ROPE // 2, QK_NOPE + QK_ROPE
    c_tab = jnp.where(lane < x1_lo, 1.0, jnp.where(lane < x2_hi, cosv, 0.0))
    s_dn = jnp.where((lane >= x2_lo) & (lane < x2_hi), sinv, 0.0)
    s_up = jnp.where((lane >= x1_lo) & (lane < x2_lo), -sinv, 0.0)
    half = QK_ROPE // 2

    def rope(blk, c, sd, su):
        return blk * c + pltpu.roll(blk, half, 1) * sd + pltpu.roll(blk, LANES - half, 1) * su

    qn = _rms(c_q, qg_ref[...]).astype(BF16)
    qf = _dot(qn, wq_ref[...])
    cq, sdq, suq = c_tab * scale, s_dn * scale, s_up * scale
    for hd in range(MLA_HEADS):
        sl = slice(hd * HEAD_PAD, (hd + 1) * HEAD_PAD)
        q_ref[:, sl] = rope(qf[:, sl], cq, sdq, suq).astype(BF16)

    kvn = _rms(c_kv, kvg_ref[...]).astype(BF16)
    kf = _dot(kvn, wk_ref[...])
    kpe_r = rope(kpe, c_tab, s_dn, s_up)
    for hd in range(MLA_HEADS):
        sl = slice(hd * HEAD_PAD, (hd + 1) * HEAD_PAD)
        k_ref[:, sl] = (kf[:, sl] + kpe_r).astype(BF16)
    v_ref[...] = _dot(kvn, wv_ref[...]).astype(BF16)


def _ab_in(h, pos, g, w_in, qg, wq, kvg, wk, wv, invf):
    n = h.shape[0]
    tm = ROW_TILE
    row = lambda c: pl.BlockSpec((tm, c), lambda i: (i, 0))
    scale = float((QK_NOPE + QK_ROPE) ** -0.5)
    return pl.pallas_call(
        functools.partial(_ab_in_kernel, scale=scale),
        grid=(n // tm,),
        in_specs=[row(D_MODEL), row(1), _const_spec(g.shape), _const_spec(w_in.shape),
                  _const_spec(qg.shape), _const_spec(wq.shape), _const_spec(kvg.shape),
                  _const_spec(wk.shape), _const_spec(wv.shape), _const_spec(invf.shape)],
        out_specs=[row(MLA_HEADS * HEAD_PAD), row(MLA_HEADS * HEAD_PAD), row(MLA_HEADS * V_HEAD),
                   row(LRU_WIDTH), row(LRU_WIDTH)],
        out_shape=[jax.ShapeDtypeStruct((n, MLA_HEADS * HEAD_PAD), BF16),
                   jax.ShapeDtypeStruct((n, MLA_HEADS * HEAD_PAD), BF16),
                   jax.ShapeDtypeStruct((n, MLA_HEADS * V_HEAD), BF16),
                   jax.ShapeDtypeStruct((n, LRU_WIDTH), F32),
                   jax.ShapeDtypeStruct((n, LRU_WIDTH), F32)],
        compiler_params=pltpu.CompilerParams(dimension_semantics=("arbitrary",),
                                             vmem_limit_bytes=VMEM_LIMIT),
        name="ab_in",
    )(h, pos, g, w_in, qg, wq, kvg, wk, wv, invf)


def _attn_kernel(q_ref, k_ref, v_ref, o_ref):
    t = q_ref.shape[0]
    i = pl.program_id(2)
    q = (q_ref[:, :HEAD_PAD], q_ref[:, HEAD_PAD:])

    def block(j, carry, masked):
        start = pl.multiple_of(j * t, t)
        kb = k_ref[pl.ds(start, t), :]
        vb = v_ref[pl.ds(start, t), :]
        out = []
        for hd in range(2):
            m, l, acc = carry[hd]
            s = _dot_nt(q[hd], kb[:, hd * HEAD_PAD:(hd + 1) * HEAD_PAD])
            if masked:
                r = lax.broadcasted_iota(jnp.int32, (t, t), 0)
                c = lax.broadcasted_iota(jnp.int32, (t, t), 1)
                s = jnp.where(c <= r, s, -jnp.inf)
            m_new = jnp.maximum(m, jnp.max(s, axis=1, keepdims=True))
            alpha = jnp.exp(m - m_new)
            p = jnp.exp(s - m_new)
            l = alpha * l + jnp.sum(p, axis=1, keepdims=True)
            acc = alpha * acc + _dot(p.astype(BF16), vb)
            out.append((m_new, l, acc))
        return tuple(out)

    init = tuple((jnp.full((t, 1), -jnp.inf, F32), jnp.zeros((t, 1), F32),
                  jnp.zeros((t, LANES), F32)) for _ in range(2))
    carry = lax.fori_loop(0, i, lambda j, c: block(j, c, False), init)
    (_, l0, a0), (_, l1, a1) = block(i, carry, True)
    lane = lax.broadcasted_iota(jnp.int32, (t, LANES), 1)
    o_ref[...] = jnp.where(lane < V_HEAD, a0 / l0, a1 / l1).astype(o_ref.dtype)


def _attention(q, k, v):
    b, s, _ = q.shape
    t = ATTN_TILE
    pairs = MLA_HEADS // 2
    return pl.pallas_call(
        _attn_kernel,
        grid=(b, pairs, s // t),
        in_specs=[pl.BlockSpec((None, t, 2 * HEAD_PAD), lambda bi, p, i: (bi, i, p)),
                  pl.BlockSpec((None, s, 2 * HEAD_PAD), lambda bi, p, i: (bi, 0, p)),
                  pl.BlockSpec((None, s, 2 * V_HEAD), lambda bi, p, i: (bi, 0, p))],
        out_specs=pl.BlockSpec((None, t, 2 * V_HEAD), lambda bi, p, i: (bi, i, p)),
        out_shape=jax.ShapeDtypeStruct((b, s, MLA_HEADS * V_HEAD), BF16),
        compiler_params=pltpu.CompilerParams(
            dimension_semantics=("arbitrary", "arbitrary", "arbitrary"),
            vmem_limit_bytes=VMEM_LIMIT),
        name="mla_attn",
    )(q, k, v)


def _rglru_kernel(x_ref, gate_ref, cw_ref, cb_ref, wa_ref, ba_ref, wx_ref, bx_ref, lam_ref,
                  y_ref, xprev_ref, hprev_ref, a_ref, b_ref):
    ts, c = x_ref.shape

    @pl.when(pl.program_id(1) == 0)
    def _():
        xprev_ref[...] = jnp.zeros_like(xprev_ref)
        hprev_ref[...] = jnp.zeros_like(hprev_ref)

    x = x_ref[...]
    prev = xprev_ref[...]
    xprev_ref[...] = x[ts - SUBLANES:]
    xc = cb_ref[...] + cw_ref[LRU_CONV - 1:LRU_CONV, :] * x
    for k in range(1, LRU_CONV):
        xc = xc + cw_ref[LRU_CONV - 1 - k:LRU_CONV - k, :] * _shift_rows(x, prev, k)

    xb = xc.astype(BF16)
    r = jax.nn.sigmoid(_dot(xb, wa_ref[...]) + ba_ref[...])
    gi = jax.nn.sigmoid(_dot(xb, wx_ref[...]) + bx_ref[...])
    z = -lam_ref[...]
    softplus = jnp.maximum(z, 0.0) + jnp.log1p(jnp.exp(-jnp.abs(z)))
    log_a = (-LRU_C) * r * softplus
    a_ref[...] = jnp.exp(log_a)
    b_ref[...] = jnp.sqrt(-_expm1(2.0 * log_a)) * (gi * xc)

    row = lax.broadcasted_iota(jnp.int32, (SUBLANES, c), 0)

    def group(g, hprev):
        r0 = pl.multiple_of(g * SUBLANES, SUBLANES)
        a = a_ref[pl.ds(r0, SUBLANES), :]
        b = b_ref[pl.ds(r0, SUBLANES), :]
        for k in (1, 2, 4):
            a_sh = jnp.where(row >= k, pltpu.roll(a, k, 0), 1.0)
            b_sh = jnp.where(row >= k, pltpu.roll(b, k, 0), 0.0)
            b = a * b_sh + b
            a = a * a_sh
        h = a * hprev + b
        b_ref[pl.ds(r0, SUBLANES), :] = h
        return jnp.broadcast_to(h[SUBLANES - 1:, :], (SUBLANES, c))

    hprev_ref[...] = lax.fori_loop(0, ts // SUBLANES, group, hprev_ref[...])
    y_ref[...] = (b_ref[...] * _gelu(gate_ref[...])).astype(y_ref.dtype)


def _rglru(x, gate, cw, cb, wa, ba, wx, bx, lam):
    b, s, c = x.shape
    ts = LRU_TILE
    seq = pl.BlockSpec((None, ts, c), lambda bi, i: (bi, i, 0))
    return pl.pallas_call(
        _rglru_kernel,
        grid=(b, s // ts),
        in_specs=[seq, seq, _const_spec(cw.shape), _const_spec(cb.shape), _const_spec(wa.shape),
                  _const_spec(ba.shape), _const_spec(wx.shape), _const_spec(bx.shape),
                  _const_spec(lam.shape)],
        out_specs=seq,
        out_shape=jax.ShapeDtypeStruct((b, s, c), BF16),
        scratch_shapes=[pltpu.VMEM((SUBLANES, c), F32), pltpu.VMEM((SUBLANES, c), F32),
                        pltpu.VMEM((ts, c), F32), pltpu.VMEM((ts, c), F32)],
        compiler_params=pltpu.CompilerParams(dimension_semantics=("arbitrary", "arbitrary"),
                                             vmem_limit_bytes=VMEM_LIMIT),
        name="rglru",
    )(x, gate, cw, cb, wa, ba, wx, bx, lam)


def _ffn(h1, seq_tile, g_ref, wg_ref, wu_ref, cw_ref, cb_ref, wd_ref, hn_ref, acc_ref, carry_ref):
    tm = h1.shape[0]
    hn_ref[...] = _rms(h1, g_ref[...]).astype(BF16)
    acc_ref[...] = h1

    @pl.when(seq_tile == 0)
    def _():
        carry_ref[...] = jnp.zeros_like(carry_ref)

    def chunk(f, _):
        hn = hn_ref[...]
        g = _dot(hn, wg_ref[f])
        u = _dot(hn, wu_ref[f])
        prev = carry_ref[f]
        carry_ref[f] = g[tm - SUBLANES:]
        cw = cw_ref[f]
        y = cb_ref[f] + cw[FFN_CONV - 1:FFN_CONV] * g
        for k in range(1, FFN_CONV):
            y = y + cw[FFN_CONV - 1 - k:FFN_CONV - k] * _shift_rows(g, prev, k)
        act = (_gelu(y) * u).astype(BF16)
        acc_ref[...] += _dot(act, wd_ref[f])
        return 0

    lax.fori_loop(0, N_FF_CHUNKS, chunk, 0)
    return acc_ref[...]


def _ffn_scratch(tm):
    return [pltpu.VMEM((tm, D_MODEL), BF16), pltpu.VMEM((tm, D_MODEL), F32),
            pltpu.VMEM((N_FF_CHUNKS, SUBLANES, FF_CHUNK), F32)]


def _ab_out_ffn_kernel(h_ref, ya_ref, yb_ref, woa_ref, wob_ref, g_ref, wg_ref, wu_ref, cw_ref,
                       cb_ref, wd_ref, o_ref, hn_ref, acc_ref, carry_ref, *, tiles_per_seq):
    seq_tile = pl.program_id(0) % tiles_per_seq
    h1 = h_ref[...] + _dot(ya_ref[...], woa_ref[...]) + _dot(yb_ref[...], wob_ref[...])
    o_ref[...] = _ffn(h1, seq_tile, g_ref, wg_ref, wu_ref, cw_ref, cb_ref, wd_ref,
                      hn_ref, acc_ref, carry_ref)


def _ab_out_ffn(h, ya, yb, woa, wob, g, wg, wu, cw, cb, wd, seq_len):
    n = h.shape[0]
    tm = ROW_TILE
    row = lambda c: pl.BlockSpec((tm, c), lambda i: (i, 0))
    consts = (woa, wob, g, wg, wu, cw, cb, wd)
    return pl.pallas_call(
        functools.partial(_ab_out_ffn_kernel, tiles_per_seq=seq_len // tm),
        grid=(n // tm,),
        in_specs=[row(D_MODEL), row(ya.shape[1]), row(yb.shape[1])]
                 + [_const_spec(a.shape) for a in consts],
        out_specs=row(D_MODEL),
        out_shape=jax.ShapeDtypeStruct((n, D_MODEL), F32),
        scratch_shapes=_ffn_scratch(tm),
        compiler_params=pltpu.CompilerParams(dimension_semantics=("arbitrary",),
                                             vmem_limit_bytes=VMEM_LIMIT),
        name="ab_out_ffn",
    )(h, ya, yb, *consts)


def _sgu_ffn_kernel(h_ref, cg_ref, win_ref, lng_ref, lnb_ref, ws_ref, bs_ref, wout_ref,
                    g_ref, wg_ref, wu_ref, cw_ref, cb_ref, wd_ref, fg_ref, o_ref,
                    u_ref, v_ref, gated_ref, hn_ref, acc_ref, carry_ref, *, tiles_per_seq):
    tm = h_ref.shape[0]
    seq_tile = pl.program_id(0) % tiles_per_seq
    h = h_ref[...]
    xn = _rms(h, cg_ref[...]).astype(BF16)
    u_ref[...] = _gelu(_dot(xn, win_ref[:, :D_MODEL]))
    v = _gelu(_dot(xn, win_ref[:, D_MODEL:]))
    mu = jnp.mean(v, axis=-1, keepdims=True)
    vc = v - mu
    var = jnp.mean(vc * vc, axis=-1, keepdims=True)
    v_ref[...] = (vc * lax.rsqrt(var + NORM_EPS) * lng_ref[...] + lnb_ref[...]).astype(BF16)

    n_chunks = tm // CHUNK
    r = lax.broadcasted_iota(jnp.int32, (CHUNK, CHUNK), 0)
    c = lax.broadcasted_iota(jnp.int32, (CHUNK, CHUNK), 1)
    for gp in range(SGU_GROUPS):
        lanes = slice(gp * CHUNK, (gp + 1) * CHUNK)
        w = jnp.where(c <= r, ws_ref[gp], 0.0).astype(BF16)
        rhs = jnp.concatenate([v_ref[ck * CHUNK:(ck + 1) * CHUNK, lanes] for ck in range(n_chunks)],
                              axis=1)
        sg = _dot(w, rhs) + bs_ref[:, gp:gp + 1]
        for ck in range(n_chunks):
            rows = slice(ck * CHUNK, (ck + 1) * CHUNK)
            gated_ref[rows, lanes] = (u_ref[rows, lanes] * sg[:, ck * CHUNK:(ck + 1) * CHUNK]).astype(BF16)

    h1 = h + _dot(gated_ref[...], wout_ref[...])
    h2 = _ffn(h1, seq_tile, g_ref, wg_ref, wu_ref, cw_ref, cb_ref, wd_ref, hn_ref, acc_ref, carry_ref)
    o_ref[...] = _rms(h2, fg_ref[...])


def _sgu_ffn(h, cg, win, lng, lnb, ws, bs_t, wout, g, wg, wu, cw, cb, wd, fg, seq_len):
    n = h.shape[0]
    tm = ROW_TILE
    row = lambda c: pl.BlockSpec((tm, c), lambda i: (i, 0))
    consts = (cg, win, lng, lnb, ws, bs_t, wout, g, wg, wu, cw, cb, wd, fg)
    return pl.pallas_call(
        functools.partial(_sgu_ffn_kernel, tiles_per_seq=seq_len // tm),
        grid=(n // tm,),
        in_specs=[row(D_MODEL)] + [_const_spec(a.shape) for a in consts],
        out_specs=row(D_MODEL),
        out_shape=jax.ShapeDtypeStruct((n, D_MODEL), F32),
        scratch_shapes=[pltpu.VMEM((tm, D_MODEL), F32), pltpu.VMEM((tm, D_MODEL), BF16),
                        pltpu.VMEM((tm, D_MODEL), BF16)] + _ffn_scratch(tm),
        compiler_params=pltpu.CompilerParams(dimension_semantics=("arbitrary",),
                                             vmem_limit_bytes=VMEM_LIMIT),
        name="sgu_ffn",
    )(h, *consts)


def _ffn_params(w_gate, w_up, conv_w, conv_b, w_down):
    wg = w_gate.astype(BF16).reshape(D_MODEL, N_FF_CHUNKS, FF_CHUNK).transpose(1, 0, 2)
    wu = w_up.astype(BF16).reshape(D_MODEL, N_FF_CHUNKS, FF_CHUNK).transpose(1, 0, 2)
    cw = conv_w.reshape(FFN_CONV, N_FF_CHUNKS, FF_CHUNK).transpose(1, 0, 2)
    cb = conv_b.reshape(N_FF_CHUNKS, 1, FF_CHUNK)
    wd = w_down.astype(BF16).reshape(N_FF_CHUNKS, FF_CHUNK, D_MODEL)
    return wg, wu, cw, cb, wd


def _block_diag(w):
    heads, blk, _ = w.shape
    eye = jnp.eye(heads, dtype=w.dtype)
    return (w[:, :, None, :] * eye[:, None, :, None]).reshape(heads * blk, heads * blk)


def kernel(x, positions, ab_norm, ab_w_in, ab_q_norm, ab_w_q_b, ab_kv_norm, ab_w_kv_b, ab_conv_w, ab_conv_b, ab_w_rg_a, ab_b_rg_a, ab_w_rg_x, ab_b_rg_x, ab_lambda, ab_w_out, c_norm, c_w_in, c_ln_g, c_ln_b, c_w_s, c_b_s, c_w_out, ffn_norm, ffn_w_gate, ffn_w_up, ffn_conv_w, ffn_conv_b, ffn_w_down, final_norm):
    b, s, d = x.shape
    n = b * s
    h = x.reshape(n, d)
    pos = positions.reshape(n, 1)

    w_in = ab_w_in[0]
    o2 = Q_LORA + KV_LORA
    o3 = o2 + QK_ROPE
    zeros = lambda c: jnp.zeros((d, c), w_in.dtype)
    w_in_p = jnp.concatenate([w_in[:, :o2], zeros(QK_NOPE), w_in[:, o2:o3],
                              zeros(HEAD_PAD - QK_NOPE - QK_ROPE), w_in[:, o3:]], axis=1).astype(BF16)
    qk = QK_NOPE + QK_ROPE
    wq = jnp.pad(ab_w_q_b[0].reshape(Q_LORA, MLA_HEADS, qk),
                 ((0, 0), (0, 0), (0, HEAD_PAD - qk))).reshape(Q_LORA, MLA_HEADS * HEAD_PAD).astype(BF16)
    wkv = ab_w_kv_b[0].reshape(KV_LORA, MLA_HEADS, QK_NOPE + V_HEAD)
    wk = jnp.pad(wkv[:, :, :QK_NOPE], ((0, 0), (0, 0), (0, HEAD_PAD - QK_NOPE))
                 ).reshape(KV_LORA, MLA_HEADS * HEAD_PAD).astype(BF16)
    wv = wkv[:, :, QK_NOPE:].reshape(KV_LORA, MLA_HEADS * V_HEAD).astype(BF16)
    half = QK_ROPE // 2
    freq = jnp.exp(-math.log(ROPE_BASE) * jnp.arange(half, dtype=F32) / half)
    invf = jnp.concatenate([jnp.zeros((QK_NOPE,), F32), freq, freq,
                            jnp.zeros((HEAD_PAD - QK_NOPE - QK_ROPE,), F32)]).reshape(1, HEAD_PAD)

    q, k, v, x_lru, gate_lru = _ab_in(
        h, pos, ab_norm[0].reshape(1, d), w_in_p, ab_q_norm[0].reshape(1, Q_LORA), wq,
        ab_kv_norm[0].reshape(1, KV_LORA), wk, wv, invf)

    y_mla = _attention(q.reshape(b, s, -1), k.reshape(b, s, -1), v.reshape(b, s, -1))
    y_lru = _rglru(x_lru.reshape(b, s, LRU_WIDTH), gate_lru.reshape(b, s, LRU_WIDTH),
                   ab_conv_w[0], ab_conv_b[0].reshape(1, -1),
                   _block_diag(ab_w_rg_a[0]).astype(BF16), ab_b_rg_a[0].reshape(1, -1),
                   _block_diag(ab_w_rg_x[0]).astype(BF16), ab_b_rg_x[0].reshape(1, -1),
                   ab_lambda[0].reshape(1, -1))

    w_out = ab_w_out[0].astype(BF16)
    mla_w = MLA_HEADS * V_HEAD
    h = _ab_out_ffn(h, y_mla.reshape(n, mla_w), y_lru.reshape(n, LRU_WIDTH),
                    w_out[:mla_w], w_out[mla_w:], ffn_norm[0].reshape(1, d),
                    *_ffn_params(ffn_w_gate[0], ffn_w_up[0], ffn_conv_w[0], ffn_conv_b[0],
                                 ffn_w_down[0]), seq_len=s)

    out = _sgu_ffn(h, c_norm[0].reshape(1, d), c_w_in[0].astype(BF16), c_ln_g[0].reshape(1, -1),
                   c_ln_b[0].reshape(1, -1), c_w_s[0], c_b_s[0].T, c_w_out[0].astype(BF16),
                   ffn_norm[1].reshape(1, d),
                   *_ffn_params(ffn_w_gate[1], ffn_w_up[1], ffn_conv_w[1], ffn_conv_b[1],
                                ffn_w_down[1]), final_norm.reshape(1, d), seq_len=s)
    return out.reshape(b, s, d)
```

```python
import functools
import math

import jax
import jax.numpy as jnp
from jax import lax
from jax.experimental import pallas as pl
from jax.experimental.pallas import tpu as pltpu

F32 = jnp.float32
BF16 = jnp.bfloat16

D_MODEL = 1024
MLA_HEADS = 8
Q_LORA = 256
KV_LORA = 128
QK_NOPE = 64
QK_ROPE = 32
V_HEAD = 64
ROPE_BASE = 10000.0
LRU_WIDTH = 512
LRU_HEADS = 8
LRU_CONV = 4
LRU_C = 8.0
CHUNK = 128
SGU_GROUPS = 8
D_FF = 2816
FFN_CONV = 3
NORM_EPS = 1e-6

LANES = 128
SUBLANES = 8
HEAD_PAD = 128
AB_IN_PAD = Q_LORA + KV_LORA + HEAD_PAD + 2 * LRU_WIDTH

ROW_TILE = 512
ATTN_TILE = 256
LRU_TILE = 256
FF_CHUNK = 256
N_FF_CHUNKS = D_FF // FF_CHUNK
VMEM_LIMIT = 56 * 1024 * 1024


def _dot(a, b):
    return jnp.dot(a, b, preferred_element_type=F32)


def _dot_nt(a, b):
    return lax.dot_general(a, b, (((1,), (1,)), ((), ())), preferred_element_type=F32)


def _gelu(x):
    c = math.sqrt(2.0 / math.pi)
    return 0.5 * x * (1.0 + jnp.tanh(c * (x + 0.044715 * (x * x * x))))


def _expm1(x):
    u = jnp.exp(x)
    small = jnp.where(u == 1.0, x, (u - 1.0) * x / jnp.log(u))
    return jnp.where(x < -0.5, u - 1.0, small)


def _rms(x, g):
    ms = jnp.mean(x * x, axis=-1, keepdims=True)
    return x * lax.rsqrt(ms + NORM_EPS) * g


def _shift_rows(x, prev8, k):
    rolled = pltpu.roll(x, k, 0)
    row = lax.broadcasted_iota(jnp.int32, (SUBLANES, x.shape[1]), 0)
    top = jnp.where(row < k, pltpu.roll(prev8, k, 0), rolled[:SUBLANES])
    return jnp.concatenate([top, rolled[SUBLANES:]], axis=0)


def _const_spec(shape):
    nd = len(shape)
    return pl.BlockSpec(shape, lambda *_: (0,) * nd, pipeline_mode=pl.Buffered(1))


def _ab_in_kernel(h_ref, pos_ref, g_ref, w_in_ref, qg_ref, wq_ref, kvg_ref, wk_ref, wv_ref,
                  invf_ref, q_ref, k_ref, v_ref, xl_ref, gate_ref, *, scale):
    tm = h_ref.shape[0]
    xn = _rms(h_ref[...], g_ref[...]).astype(BF16)

    o1 = Q_LORA
    o2 = o1 + KV_LORA
    o3 = o2 + HEAD_PAD
    o4 = o3 + LRU_WIDTH
    xl_ref[...] = _dot(xn, w_in_ref[:, o3:o4])
    gate_ref[...] = _dot(xn, w_in_ref[:, o4:])
    c_q = _dot(xn, w_in_ref[:, :o1])
    c_kv = _dot(xn, w_in_ref[:, o1:o2])
    kpe = _dot(xn, w_in_ref[:, o2:o3])

    ang = pos_ref[...].astype(F32) * invf_ref[...]
    lane = lax.broadcasted_iota(jnp.int32, (tm, LANES), 1)
    cosv = jnp.cos(ang)
    sinv = jnp.sin(ang)
    x1_lo, x2_lo, x2_hi = QK_NOPE, QK_NOPE + QK_ROPE // 2, QK_NOPE + QK_ROPE
    c_tab = jnp.where(lane < x1_lo, 1.0, jnp.where(lane < x2_hi, cosv, 0.0))
    s_dn = jnp.where((lane >= x2_lo) & (lane < x2_hi), sinv, 0.0)
    s_up = jnp.where((lane >= x1_lo) & (lane < x2_lo), -sinv, 0.0)
    half = QK_ROPE // 2

    def rope(blk, c, sd, su):
        return blk * c + pltpu.roll(blk, half, 1) * sd + pltpu.roll(blk, LANES - half, 1) * su

    qn = _rms(c_q, qg_ref[...]).astype(BF16)
    qf = _dot(qn, wq_ref[...])
    cq, sdq, suq = c_tab * scale, s_dn * scale, s_up * scale
    for hd in range(MLA_HEADS):
        sl = slice(hd * HEAD_PAD, (hd + 1) * HEAD_PAD)
        q_ref[:, sl] = rope(qf[:, sl], cq, sdq, suq).astype(BF16)

    kvn = _rms(c_kv, kvg_ref[...]).astype(BF16)
    kf = _dot(kvn, wk_ref[...])
    kpe_r = rope(kpe, c_tab, s_dn, s_up)
    for hd in range(MLA_HEADS):
        sl = slice(hd * HEAD_PAD, (hd + 1) * HEAD_PAD)
        k_ref[:, sl] = (kf[:, sl] + kpe_r).astype(BF16)
    v_ref[...] = _dot(kvn, wv_ref[...]).astype(BF16)


def _ab_in(h, pos, g, w_in, qg, wq, kvg, wk, wv, invf):
    n = h.shape[0]
    tm = ROW_TILE
    row = lambda c: pl.BlockSpec((tm, c), lambda i: (i, 0))
    scale = float((QK_NOPE + QK_ROPE) ** -0.5)
    return pl.pallas_call(
        functools.partial(_ab_in_kernel, scale=scale),
        grid=(n // tm,),
        in_specs=[row(D_MODEL), row(1), _const_spec(g.shape), _const_spec(w_in.shape),
                  _const_spec(qg.shape), _const_spec(wq.shape), _const_spec(kvg.shape),
                  _const_spec(wk.shape), _const_spec(wv.shape), _const_spec(invf.shape)],
        out_specs=[row(MLA_HEADS * HEAD_PAD), row(MLA_HEADS * HEAD_PAD), row(MLA_HEADS * V_HEAD),
                   row(LRU_WIDTH), row(LRU_WIDTH)],
        out_shape=[jax.ShapeDtypeStruct((n, MLA_HEADS * HEAD_PAD), BF16),
                   jax.ShapeDtypeStruct((n, MLA_HEADS * HEAD_PAD), BF16),
                   jax.ShapeDtypeStruct((n, MLA_HEADS * V_HEAD), BF16),
                   jax.ShapeDtypeStruct((n, LRU_WIDTH), F32),
                   jax.ShapeDtypeStruct((n, LRU_WIDTH), F32)],
        compiler_params=pltpu.CompilerParams(dimension_semantics=("arbitrary",),
                                             vmem_limit_bytes=VMEM_LIMIT),
        name="ab_in",
    )(h, pos, g, w_in, qg, wq, kvg, wk, wv, invf)


def _attn_kernel(q_ref, k_ref, v_ref, o_ref):
    t = q_ref.shape[0]
    i = pl.program_id(2)
    q = (q_ref[:, :HEAD_PAD], q_ref[:, HEAD_PAD:])

    def block(j, carry, masked):
        start = pl.multiple_of(j * t, t)
        kb = k_ref[pl.ds(start, t), :]
        vb = v_ref[pl.ds(start, t), :]
        out = []
        for hd in range(2):
            m, l, acc = carry[hd]
            s = _dot_nt(q[hd], kb[:, hd * HEAD_PAD:(hd + 1) * HEAD_PAD])
            if masked:
                r = lax.broadcasted_iota(jnp.int32, (t, t), 0)
                c = lax.broadcasted_iota(jnp.int32, (t, t), 1)
                s = jnp.where(c <= r, s, -jnp.inf)
            m_new = jnp.maximum(m, jnp.max(s, axis=1, keepdims=True))
            alpha = jnp.exp(m - m_new)
            p = jnp.exp(s - m_new)
            l = alpha * l + jnp.sum(p, axis=1, keepdims=True)
            acc = alpha * acc + _dot(p.astype(BF16), vb)
            out.append((m_new, l, acc))
        return tuple(out)

    init = tuple((jnp.full((t, 1), -jnp.inf, F32), jnp.zeros((t, 1), F32),
                  jnp.zeros((t, LANES), F32)) for _ in range(2))
    carry = lax.fori_loop(0, i, lambda j, c: block(j, c, False), init)
    (_, l0, a0), (_, l1, a1) = block(i, carry, True)
    lane = lax.broadcasted_iota(jnp.int32, (t, LANES), 1)
    o_ref[...] = jnp.where(lane < V_HEAD, a0 / l0, a1 / l1).astype(o_ref.dtype)


def _attention(q, k, v):
    b, s, _ = q.shape
    t = ATTN_TILE
    pairs = MLA_HEADS // 2
    return pl.pallas_call(
        _attn_kernel,
        grid=(b, pairs, s // t),
        in_specs=[pl.BlockSpec((None, t, 2 * HEAD_PAD), lambda bi, p, i: (bi, i, p)),
                  pl.BlockSpec((None, s, 2 * HEAD_PAD), lambda bi, p, i: (bi, 0, p)),
                  pl.BlockSpec((None, s, 2 * V_HEAD), lambda bi, p, i: (bi, 0, p))],
        out_specs=pl.BlockSpec((None, t, 2 * V_HEAD), lambda bi, p, i: (bi, i, p)),
        out_shape=jax.ShapeDtypeStruct((b, s, MLA_HEADS * V_HEAD), BF16),
        compiler_params=pltpu.CompilerParams(
            dimension_semantics=("arbitrary", "arbitrary", "arbitrary"),
            vmem_limit_bytes=VMEM_LIMIT),
        name="mla_attn",
    )(q, k, v)


def _rglru_kernel(x_ref, gate_ref, cw_ref, cb_ref, wa_ref, ba_ref, wx_ref, bx_ref, lam_ref,
                  y_ref, xprev_ref, hprev_ref, a_ref, b_ref):
    ts, c = x_ref.shape

    @pl.when(pl.program_id(1) == 0)
    def _():
        xprev_ref[...] = jnp.zeros_like(xprev_ref)
        hprev_ref[...] = jnp.zeros_like(hprev_ref)

    x = x_ref[...]
    prev = xprev_ref[...]
    xprev_ref[...] = x[ts - SUBLANES:]
    xc = cb_ref[...] + cw_ref[LRU_CONV - 1:LRU_CONV, :] * x
    for k in range(1, LRU_CONV):
        xc = xc + cw_ref[LRU_CONV - 1 - k:LRU_CONV - k, :] * _shift_rows(x, prev, k)

    xb = xc.astype(BF16)
    r = jax.nn.sigmoid(_dot(xb, wa_ref[...]) + ba_ref[...])
    gi = jax.nn.sigmoid(_dot(xb, wx_ref[...]) + bx_ref[...])
    z = -lam_ref[...]
    softplus = jnp.maximum(z, 0.0) + jnp.log1p(jnp.exp(-jnp.abs(z)))
    log_a = (-LRU_C) * r * softplus
    a_ref[...] = jnp.exp(log_a)
    b_ref[...] = jnp.sqrt(-_expm1(2.0 * log_a)) * (gi * xc)

    row = lax.broadcasted_iota(jnp.int32, (SUBLANES, c), 0)

    def group(g, hprev):
        r0 = pl.multiple_of(g * SUBLANES, SUBLANES)
        a = a_ref[pl.ds(r0, SUBLANES), :]
        b = b_ref[pl.ds(r0, SUBLANES), :]
        for k in (1, 2, 4):
            a_sh = jnp.where(row >= k, pltpu.roll(a, k, 0), 1.0)
            b_sh = jnp.where(row >= k, pltpu.roll(b, k, 0), 0.0)
            b = a * b_sh + b
            a = a * a_sh
        h = a * hprev + b
        b_ref[pl.ds(r0, SUBLANES), :] = h
        return jnp.broadcast_to(h[SUBLANES - 1:, :], (SUBLANES, c))

    hprev_ref[...] = lax.fori_loop(0, ts // SUBLANES, group, hprev_ref[...])
    y_ref[...] = (b_ref[...] * _gelu(gate_ref[...])).astype(y_ref.dtype)


def _rglru(x, gate, cw, cb, wa, ba, wx, bx, lam):
    b, s, c = x.shape
    ts = LRU_TILE
    seq = pl.BlockSpec((None, ts, c), lambda bi, i: (bi, i, 0))
    return pl.pallas_call(
        _rglru_kernel,
        grid=(b, s // ts),
        in_specs=[seq, seq, _const_spec(cw.shape), _const_spec(cb.shape), _const_spec(wa.shape),
                  _const_spec(ba.shape), _const_spec(wx.shape), _const_spec(bx.shape),
                  _const_spec(lam.shape)],
        out_specs=seq,
        out_shape=jax.ShapeDtypeStruct((b, s, c), BF16),
        scratch_shapes=[pltpu.VMEM((SUBLANES, c), F32), pltpu.VMEM((SUBLANES, c), F32),
                        pltpu.VMEM((ts, c), F32), pltpu.VMEM((ts, c), F32)],
        compiler_params=pltpu.CompilerParams(dimension_semantics=("arbitrary", "arbitrary"),
                                             vmem_limit_bytes=VMEM_LIMIT),
        name="rglru",
    )(x, gate, cw, cb, wa, ba, wx, bx, lam)


def _ffn(res_ref, seq_tile, g_ref, wg_ref, wu_ref, cw_ref, cb_ref, wd_ref, hn_ref, act_ref, carry_ref):
    tm = res_ref.shape[0]
    hn_ref[...] = _rms(res_ref[...], g_ref[...]).astype(BF16)

    @pl.when(seq_tile == 0)
    def _():
        carry_ref[...] = jnp.zeros_like(carry_ref)

    for f in range(N_FF_CHUNKS):
        cols = slice(f * FF_CHUNK, (f + 1) * FF_CHUNK)
        hn = hn_ref[...]
        g = _dot(hn, wg_ref[:, cols])
        u = _dot(hn, wu_ref[:, cols])
        prev = carry_ref[:, cols]
        carry_ref[:, cols] = g[tm - SUBLANES:]
        y = cb_ref[:, cols] + cw_ref[FFN_CONV - 1:FFN_CONV, cols] * g
        for k in range(1, FFN_CONV):
            y = y + cw_ref[FFN_CONV - 1 - k:FFN_CONV - k, cols] * _shift_rows(g, prev, k)
        act_ref[:, cols] = (_gelu(y) * u).astype(BF16)
    return res_ref[...] + _dot(act_ref[...], wd_ref[...])


def _ffn_scratch(tm):
    return [pltpu.VMEM((tm, D_MODEL), BF16), pltpu.VMEM((tm, D_FF), BF16),
            pltpu.VMEM((SUBLANES, D_FF), F32)]


def _ab_out_ffn_kernel(h_ref, ya_ref, yb_ref, woa_ref, wob_ref, g_ref, wg_ref, wu_ref, cw_ref,
                       cb_ref, wd_ref, o_ref, hn_ref, act_ref, carry_ref, *, tiles_per_seq):
    seq_tile = pl.program_id(0) % tiles_per_seq
    o_ref[...] = h_ref[...] + _dot(ya_ref[...], woa_ref[...]) + _dot(yb_ref[...], wob_ref[...])
    o_ref[...] = _ffn(o_ref, seq_tile, g_ref, wg_ref, wu_ref, cw_ref, cb_ref, wd_ref,
                      hn_ref, act_ref, carry_ref)


def _ab_out_ffn(h, ya, yb, woa, wob, g, wg, wu, cw, cb, wd, seq_len):
    n = h.shape[0]
    tm = ROW_TILE
    row = lambda c: pl.BlockSpec((tm, c), lambda i: (i, 0))
    consts = (woa, wob, g, wg, wu, cw, cb, wd)
    return pl.pallas_call(
        functools.partial(_ab_out_ffn_kernel, tiles_per_seq=seq_len // tm),
        grid=(n // tm,),
        in_specs=[row(D_MODEL), row(ya.shape[1]), row(yb.shape[1])]
                 + [_const_spec(a.shape) for a in consts],
        out_specs=row(D_MODEL),
        out_shape=jax.ShapeDtypeStruct((n, D_MODEL), F32),
        scratch_shapes=_ffn_scratch(tm),
        compiler_params=pltpu.CompilerParams(dimension_semantics=("arbitrary",),
                                             vmem_limit_bytes=VMEM_LIMIT),
        name="ab_out_ffn",
    )(h, ya, yb, *consts)


def _sgu_ffn_kernel(h_ref, cg_ref, win_ref, lng_ref, lnb_ref, ws_ref, bs_ref, wout_ref,
                    g_ref, wg_ref, wu_ref, cw_ref, cb_ref, wd_ref, fg_ref, o_ref,
                    u_ref, v_ref, gated_ref, hn_ref, act_ref, carry_ref, *, tiles_per_seq):
    tm = h_ref.shape[0]
    seq_tile = pl.program_id(0) % tiles_per_seq
    h = h_ref[...]
    xn = _rms(h, cg_ref[...]).astype(BF16)
    u_ref[...] = _gelu(_dot(xn, win_ref[:, :D_MODEL]))
    v = _gelu(_dot(xn, win_ref[:, D_MODEL:]))
    mu = jnp.mean(v, axis=-1, keepdims=True)
    vc = v - mu
    var = jnp.mean(vc * vc, axis=-1, keepdims=True)
    v_ref[...] = (vc * lax.rsqrt(var + NORM_EPS) * lng_ref[...] + lnb_ref[...]).astype(BF16)

    n_chunks = tm // CHUNK
    r = lax.broadcasted_iota(jnp.int32, (CHUNK, CHUNK), 0)
    c = lax.broadcasted_iota(jnp.int32, (CHUNK, CHUNK), 1)
    for gp in range(SGU_GROUPS):
        lanes = slice(gp * CHUNK, (gp + 1) * CHUNK)
        w = jnp.where(c <= r, ws_ref[gp], 0.0).astype(BF16)
        rhs = jnp.concatenate([v_ref[ck * CHUNK:(ck + 1) * CHUNK, lanes] for ck in range(n_chunks)],
                              axis=1)
        sg = _dot(w, rhs) + bs_ref[:, gp:gp + 1]
        for ck in range(n_chunks):
            rows = slice(ck * CHUNK, (ck + 1) * CHUNK)
            gated_ref[rows, lanes] = (u_ref[rows, lanes] * sg[:, ck * CHUNK:(ck + 1) * CHUNK]).astype(BF16)

    o_ref[...] = h_ref[...] + _dot(gated_ref[...], wout_ref[...])
    h2 = _ffn(o_ref, seq_tile, g_ref, wg_ref, wu_ref, cw_ref, cb_ref, wd_ref, hn_ref, act_ref, carry_ref)
    o_ref[...] = _rms(h2, fg_ref[...])


def _sgu_ffn(h, cg, win, lng, lnb, ws, bs_t, wout, g, wg, wu, cw, cb, wd, fg, seq_len):
    n = h.shape[0]
    tm = ROW_TILE
    row = lambda c: pl.BlockSpec((tm, c), lambda i: (i, 0))
    consts = (cg, win, lng, lnb, ws, bs_t, wout, g, wg, wu, cw, cb, wd, fg)
    return pl.pallas_call(
        functools.partial(_sgu_ffn_kernel, tiles_per_seq=seq_len // tm),
        grid=(n // tm,),
        in_specs=[row(D_MODEL)] + [_const_spec(a.shape) for a in consts],
        out_specs=row(D_MODEL),
        out_shape=jax.ShapeDtypeStruct((n, D_MODEL), F32),
        scratch_shapes=[pltpu.VMEM((tm, D_MODEL), F32), pltpu.VMEM((tm, D_MODEL), BF16),
                        pltpu.VMEM((tm, D_MODEL), BF16)] + _ffn_scratch(tm),
        compiler_params=pltpu.CompilerParams(dimension_semantics=("arbitrary",),
                                             vmem_limit_bytes=VMEM_LIMIT),
        name="sgu_ffn",
    )(h, *consts)


def _ffn_params(w_gate, w_up, conv_w, conv_b, w_down):
    return (w_gate.astype(BF16), w_up.astype(BF16), conv_w, conv_b.reshape(1, D_FF),
            w_down.astype(BF16))


def _block_diag(w):
    heads, blk, _ = w.shape
    eye = jnp.eye(heads, dtype=w.dtype)
    return (w[:, :, None, :] * eye[:, None, :, None]).reshape(heads * blk, heads * blk)


def kernel(x, positions, ab_norm, ab_w_in, ab_q_norm, ab_w_q_b, ab_kv_norm, ab_w_kv_b, ab_conv_w, ab_conv_b, ab_w_rg_a, ab_b_rg_a, ab_w_rg_x, ab_b_rg_x, ab_lambda, ab_w_out, c_norm, c_w_in, c_ln_g, c_ln_b, c_w_s, c_b_s, c_w_out, ffn_norm, ffn_w_gate, ffn_w_up, ffn_conv_w, ffn_conv_b, ffn_w_down, final_norm):
    b, s, d = x.shape
    n = b * s
    h = x.reshape(n, d)
    pos = positions.reshape(n, 1)

    w_in = ab_w_in[0]
    o2 = Q_LORA + KV_LORA
    o3 = o2 + QK_ROPE
    zeros = lambda c: jnp.zeros((d, c), w_in.dtype)
    w_in_p = jnp.concatenate([w_in[:, :o2], zeros(QK_NOPE), w_in[:, o2:o3],
                              zeros(HEAD_PAD - QK_NOPE - QK_ROPE), w_in[:, o3:]], axis=1).astype(BF16)
    qk = QK_NOPE + QK_ROPE
    wq = jnp.pad(ab_w_q_b[0].reshape(Q_LORA, MLA_HEADS, qk),
                 ((0, 0), (0, 0), (0, HEAD_PAD - qk))).reshape(Q_LORA, MLA_HEADS * HEAD_PAD).astype(BF16)
    wkv = ab_w_kv_b[0].reshape(KV_LORA, MLA_HEADS, QK_NOPE + V_HEAD)
    wk = jnp.pad(wkv[:, :, :QK_NOPE], ((0, 0), (0, 0), (0, HEAD_PAD - QK_NOPE))
                 ).reshape(KV_LORA, MLA_HEADS * HEAD_PAD).astype(BF16)
    wv = wkv[:, :, QK_NOPE:].reshape(KV_LORA, MLA_HEADS * V_HEAD).astype(BF16)
    half = QK_ROPE // 2
    freq = jnp.exp(-math.log(ROPE_BASE) * jnp.arange(half, dtype=F32) / half)
    invf = jnp.concatenate([jnp.zeros((QK_NOPE,), F32), freq, freq,
                            jnp.zeros((HEAD_PAD - QK_NOPE - QK_ROPE,), F32)]).reshape(1, HEAD_PAD)

    q, k, v, x_lru, gate_lru = _ab_in(
        h, pos, ab_norm[0].reshape(1, d), w_in_p, ab_q_norm[0].reshape(1, Q_LORA), wq,
        ab_kv_norm[0].reshape(1, KV_LORA), wk, wv, invf)

    y_mla = _attention(q.reshape(b, s, -1), k.reshape(b, s, -1), v.reshape(b, s, -1))
    y_lru = _rglru(x_lru.reshape(b, s, LRU_WIDTH), gate_lru.reshape(b, s, LRU_WIDTH),
                   ab_conv_w[0], ab_conv_b[0].reshape(1, -1),
                   _block_diag(ab_w_rg_a[0]).astype(BF16), ab_b_rg_a[0].reshape(1, -1),
                   _block_diag(ab_w_rg_x[0]).astype(BF16), ab_b_rg_x[0].reshape(1, -1),
                   ab_lambda[0].reshape(1, -1))

    w_out = ab_w_out[0].astype(BF16)
    mla_w = MLA_HEADS * V_HEAD
    h = _ab_out_ffn(h, y_mla.reshape(n, mla_w), y_lru.reshape(n, LRU_WIDTH),
                    w_out[:mla_w], w_out[mla_w:], ffn_norm[0].reshape(1, d),
                    *_ffn_params(ffn_w_gate[0], ffn_w_up[0], ffn_conv_w[0], ffn_conv_b[0],
                                 ffn_w_down[0]), seq_len=s)

    out = _sgu_ffn(h, c_norm[0].reshape(1, d), c_w_in[0].astype(BF16), c_ln_g[0].reshape(1, -1),
                   c_ln_b[0].reshape(1, -1), c_w_s[0], c_b_s[0].T, c_w_out[0].astype(BF16),
                   ffn_norm[1].reshape(1, d),
                   *_ffn_params(ffn_w_gate[1], ffn_w_up[1], ffn_conv_w[1], ffn_conv_b[1],
                                ffn_w_down[1]), final_norm.reshape(1, d), seq_len=s)
    return out.reshape(b, s, d)
```

```python
import functools
import math

import jax
import jax.numpy as jnp
from jax import lax
from jax.experimental import pallas as pl
from jax.experimental.pallas import tpu as pltpu

F32 = jnp.float32
BF16 = jnp.bfloat16

D_MODEL = 1024
MLA_HEADS = 8
Q_LORA = 256
KV_LORA = 128
QK_NOPE = 64
QK_ROPE = 32
V_HEAD = 64
ROPE_BASE = 10000.0
LRU_WIDTH = 512
LRU_HEADS = 8
LRU_CONV = 4
LRU_C = 8.0
CHUNK = 128
SGU_GROUPS = 8
D_FF = 2816
FFN_CONV = 3
NORM_EPS = 1e-6

LANES = 128
SUBLANES = 8
HEAD_PAD = 128
AB_IN_PAD = Q_LORA + KV_LORA + HEAD_PAD + 2 * LRU_WIDTH

ROW_TILE = 512
ATTN_TILE = 256
LRU_TILE = 256
FF_CHUNK = 256
N_FF_CHUNKS = D_FF // FF_CHUNK
VMEM_LIMIT = 56 * 1024 * 1024


def _dot(a, b):
    return jnp.dot(a, b, preferred_element_type=F32)


def _dot_nt(a, b):
    return lax.dot_general(a, b, (((1,), (1,)), ((), ())), preferred_element_type=F32)


def _gelu(x):
    c = math.sqrt(2.0 / math.pi)
    return 0.5 * x * (1.0 + jnp.tanh(c * (x + 0.044715 * (x * x * x))))


def _expm1(x):
    u = jnp.exp(x)
    small = jnp.where(u == 1.0, x, (u - 1.0) * x / jnp.log(u))
    return jnp.where(x < -0.5, u - 1.0, small)


def _rms(x, g):
    ms = jnp.mean(x * x, axis=-1, keepdims=True)
    return x * lax.rsqrt(ms + NORM_EPS) * g


def _shift_rows(x, prev8, k):
    rolled = pltpu.roll(x, k, 0)
    row = lax.broadcasted_iota(jnp.int32, (SUBLANES, x.shape[1]), 0)
    top = jnp.where(row < k, pltpu.roll(prev8, k, 0), rolled[:SUBLANES])
    return jnp.concatenate([top, rolled[SUBLANES:]], axis=0)


def _const_spec(shape):
    nd = len(shape)
    return pl.BlockSpec(shape, lambda *_: (0,) * nd, pipeline_mode=pl.Buffered(1))


def _ab_in_kernel(h_ref, pos_ref, g_ref, w_in_ref, qg_ref, wq_ref, kvg_ref, wk_ref, wv_ref,
                  invf_ref, q_ref, k_ref, v_ref, xl_ref, gate_ref, *, scale):
    tm = h_ref.shape[0]
    xn = _rms(h_ref[...], g_ref[...]).astype(BF16)

    o1 = Q_LORA
    o2 = o1 + KV_LORA
    o3 = o2 + HEAD_PAD
    o4 = o3 + LRU_WIDTH
    xl_ref[...] = _dot(xn, w_in_ref[:, o3:o4])
    gate_ref[...] = _dot(xn, w_in_ref[:, o4:])
    c_q = _dot(xn, w_in_ref[:, :o1])
    c_kv = _dot(xn, w_in_ref[:, o1:o2])
    kpe = _dot(xn, w_in_ref[:, o2:o3])

    ang = pos_ref[...].astype(F32) * invf_ref[...]
    lane = lax.broadcasted_iota(jnp.int32, (tm, LANES), 1)
    cosv = jnp.cos(ang)
    sinv = jnp.sin(ang)
    x1_lo, x2_lo, x2_hi = QK_NOPE, QK_NOPE + QK_ROPE // 2, QK_NOPE + QK_ROPE
    c_tab = jnp.where(lane < x1_lo, 1.0, jnp.where(lane < x2_hi, cosv, 0.0))
    s_dn = jnp.where((lane >= x2_lo) & (lane < x2_hi), sinv, 0.0)
    s_up = jnp.where((lane >= x1_lo) & (lane < x2_lo), -sinv, 0.0)
    half = QK_ROPE // 2

    def rope(blk, c, sd, su):
        return blk * c + pltpu.roll(blk, half, 1) * sd + pltpu.roll(blk, LANES - half, 1) * su

    qn = _rms(c_q, qg_ref[...]).astype(BF16)
    qf = _dot(qn, wq_ref[...])
    cq, sdq, suq = c_tab * scale, s_dn * scale, s_up * scale
    for hd in range(MLA_HEADS):
        sl = slice(hd * HEAD_PAD, (hd + 1) * HEAD_PAD)
        q_ref[:, sl] = rope(qf[:, sl], cq, sdq, suq).astype(BF16)

    kvn = _rms(c_kv, kvg_ref[...]).astype(BF16)
    kf = _dot(kvn, wk_ref[...])
    kpe_r = rope(kpe, c_tab, s_dn, s_up)
    for hd in range(MLA_HEADS):
        sl = slice(hd * HEAD_PAD, (hd + 1) * HEAD_PAD)
        k_ref[:, sl] = (kf[:, sl] + kpe_r).astype(BF16)
    v_ref[...] = _dot(kvn, wv_ref[...]).astype(BF16)


def _ab_in(h, pos, g, w_in, qg, wq, kvg, wk, wv, invf):
    n = h.shape[0]
    tm = ROW_TILE
    row = lambda c: pl.BlockSpec((tm, c), lambda i: (i, 0))
    scale = float((QK_NOPE + QK_ROPE) ** -0.5)
    return pl.pallas_call(
        functools.partial(_ab_in_kernel, scale=scale),
        grid=(n // tm,),
        in_specs=[row(D_MODEL), row(1), _const_spec(g.shape), _const_spec(w_in.shape),
                  _const_spec(qg.shape), _const_spec(wq.shape), _const_spec(kvg.shape),
                  _const_spec(wk.shape), _const_spec(wv.shape), _const_spec(invf.shape)],
        out_specs=[row(MLA_HEADS * HEAD_PAD), row(MLA_HEADS * HEAD_PAD), row(MLA_HEADS * V_HEAD),
                   row(LRU_WIDTH), row(LRU_WIDTH)],
        out_shape=[jax.ShapeDtypeStruct((n, MLA_HEADS * HEAD_PAD), BF16),
                   jax.ShapeDtypeStruct((n, MLA_HEADS * HEAD_PAD), BF16),
                   jax.ShapeDtypeStruct((n, MLA_HEADS * V_HEAD), BF16),
                   jax.ShapeDtypeStruct((n, LRU_WIDTH), F32),
                   jax.ShapeDtypeStruct((n, LRU_WIDTH), F32)],
        compiler_params=pltpu.CompilerParams(dimension_semantics=("arbitrary",),
                                             vmem_limit_bytes=VMEM_LIMIT),
        name="ab_in",
    )(h, pos, g, w_in, qg, wq, kvg, wk, wv, invf)


def _attn_kernel(q_ref, k_ref, v_ref, o_ref):
    s_len = q_ref.shape[0]
    t = ATTN_TILE
    r = lax.broadcasted_iota(jnp.int32, (t, t), 0)
    c = lax.broadcasted_iota(jnp.int32, (t, t), 1)
    causal = c <= r
    lane = lax.broadcasted_iota(jnp.int32, (t, LANES), 1)
    for i in range(s_len // t):
        rows = slice(i * t, (i + 1) * t)
        kv = (i + 1) * t
        outs = []
        for hd in range(2):
            hl = slice(hd * HEAD_PAD, (hd + 1) * HEAD_PAD)
            s = _dot_nt(q_ref[rows, hl], k_ref[:kv, hl])
            diag = jnp.where(causal, s[:, kv - t:], -jnp.inf)
            s = diag if i == 0 else jnp.concatenate([s[:, :kv - t], diag], axis=1)
            m = jnp.max(s, axis=1, keepdims=True)
            p = jnp.exp(s - m)
            l = jnp.sum(p, axis=1, keepdims=True)
            outs.append(_dot(p.astype(BF16), v_ref[:kv, :]) / l)
        o_ref[rows, :] = jnp.where(lane < V_HEAD, outs[0], outs[1]).astype(o_ref.dtype)


def _attention(q, k, v):
    b, s, _ = q.shape
    pairs = MLA_HEADS // 2
    return pl.pallas_call(
        _attn_kernel,
        grid=(b, pairs),
        in_specs=[pl.BlockSpec((None, s, 2 * HEAD_PAD), lambda bi, p: (bi, 0, p)),
                  pl.BlockSpec((None, s, 2 * HEAD_PAD), lambda bi, p: (bi, 0, p)),
                  pl.BlockSpec((None, s, 2 * V_HEAD), lambda bi, p: (bi, 0, p))],
        out_specs=pl.BlockSpec((None, s, 2 * V_HEAD), lambda bi, p: (bi, 0, p)),
        out_shape=jax.ShapeDtypeStruct((b, s, MLA_HEADS * V_HEAD), BF16),
        compiler_params=pltpu.CompilerParams(dimension_semantics=("arbitrary", "arbitrary"),
                                             vmem_limit_bytes=VMEM_LIMIT),
        name="mla_attn",
    )(q, k, v)


def _rglru_kernel(x_ref, gate_ref, cw_ref, cb_ref, wa_ref, ba_ref, wx_ref, bx_ref, lam_ref,
                  y_ref, xprev_ref, hprev_ref, a_ref, b_ref):
    ts, c = x_ref.shape

    @pl.when(pl.program_id(1) == 0)
    def _():
        xprev_ref[...] = jnp.zeros_like(xprev_ref)
        hprev_ref[...] = jnp.zeros_like(hprev_ref)

    x = x_ref[...]
    prev = xprev_ref[...]
    xprev_ref[...] = x[ts - SUBLANES:]
    xc = cb_ref[...] + cw_ref[LRU_CONV - 1:LRU_CONV, :] * x
    for k in range(1, LRU_CONV):
        xc = xc + cw_ref[LRU_CONV - 1 - k:LRU_CONV - k, :] * _shift_rows(x, prev, k)

    xb = xc.astype(BF16)
    r = jax.nn.sigmoid(_dot(xb, wa_ref[...]) + ba_ref[...])
    gi = jax.nn.sigmoid(_dot(xb, wx_ref[...]) + bx_ref[...])
    z = -lam_ref[...]
    softplus = jnp.maximum(z, 0.0) + jnp.log1p(jnp.exp(-jnp.abs(z)))
    log_a = (-LRU_C) * r * softplus
    a_ref[...] = jnp.exp(log_a)
    b_ref[...] = jnp.sqrt(-_expm1(2.0 * log_a)) * (gi * xc)

    row = lax.broadcasted_iota(jnp.int32, (SUBLANES, c), 0)

    def group(g, hprev):
        r0 = pl.multiple_of(g * SUBLANES, SUBLANES)
        a = a_ref[pl.ds(r0, SUBLANES), :]
        b = b_ref[pl.ds(r0, SUBLANES), :]
        for k in (1, 2, 4):
            a_sh = jnp.where(row >= k, pltpu.roll(a, k, 0), 1.0)
            b_sh = jnp.where(row >= k, pltpu.roll(b, k, 0), 0.0)
            b = a * b_sh + b
            a = a * a_sh
        h = a * hprev + b
        b_ref[pl.ds(r0, SUBLANES), :] = h
        return jnp.broadcast_to(h[SUBLANES - 1:, :], (SUBLANES, c))

    hprev_ref[...] = lax.fori_loop(0, ts // SUBLANES, group, hprev_ref[...])
    y_ref[...] = (b_ref[...] * _gelu(gate_ref[...])).astype(y_ref.dtype)


def _rglru(x, gate, cw, cb, wa, ba, wx, bx, lam):
    b, s, c = x.shape
    ts = LRU_TILE
    seq = pl.BlockSpec((None, ts, c), lambda bi, i: (bi, i, 0))
    return pl.pallas_call(
        _rglru_kernel,
        grid=(b, s // ts),
        in_specs=[seq, seq, _const_spec(cw.shape), _const_spec(cb.shape), _const_spec(wa.shape),
                  _const_spec(ba.shape), _const_spec(wx.shape), _const_spec(bx.shape),
                  _const_spec(lam.shape)],
        out_specs=seq,
        out_shape=jax.ShapeDtypeStruct((b, s, c), BF16),
        scratch_shapes=[pltpu.VMEM((SUBLANES, c), F32), pltpu.VMEM((SUBLANES, c), F32),
                        pltpu.VMEM((ts, c), F32), pltpu.VMEM((ts, c), F32)],
        compiler_params=pltpu.CompilerParams(dimension_semantics=("arbitrary", "arbitrary"),
                                             vmem_limit_bytes=VMEM_LIMIT),
        name="rglru",
    )(x, gate, cw, cb, wa, ba, wx, bx, lam)


def _ffn(res_ref, seq_tile, g_ref, wg_ref, wu_ref, cw_ref, cb_ref, wd_ref, hn_ref, act_ref, carry_ref):
    tm = res_ref.shape[0]
    hn_ref[...] = _rms(res_ref[...], g_ref[...]).astype(BF16)

    @pl.when(seq_tile == 0)
    def _():
        carry_ref[...] = jnp.zeros_like(carry_ref)

    for f in range(N_FF_CHUNKS):
        cols = slice(f * FF_CHUNK, (f + 1) * FF_CHUNK)
        hn = hn_ref[...]
        g = _dot(hn, wg_ref[:, cols])
        u = _dot(hn, wu_ref[:, cols])
        prev = carry_ref[:, cols]
        carry_ref[:, cols] = g[tm - SUBLANES:]
        y = cb_ref[:, cols] + cw_ref[FFN_CONV - 1:FFN_CONV, cols] * g
        for k in range(1, FFN_CONV):
            y = y + cw_ref[FFN_CONV - 1 - k:FFN_CONV - k, cols] * _shift_rows(g, prev, k)
        act_ref[:, cols] = (_gelu(y) * u).astype(BF16)
    return res_ref[...] + _dot(act_ref[...], wd_ref[...])


def _ffn_scratch(tm):
    return [pltpu.VMEM((tm, D_MODEL), BF16), pltpu.VMEM((tm, D_FF), BF16),
            pltpu.VMEM((SUBLANES, D_FF), F32)]


def _ab_out_ffn_kernel(h_ref, ya_ref, yb_ref, woa_ref, wob_ref, g_ref, wg_ref, wu_ref, cw_ref,
                       cb_ref, wd_ref, o_ref, hn_ref, act_ref, carry_ref, *, tiles_per_seq):
    seq_tile = pl.program_id(0) % tiles_per_seq
    o_ref[...] = h_ref[...] + _dot(ya_ref[...], woa_ref[...]) + _dot(yb_ref[...], wob_ref[...])
    o_ref[...] = _ffn(o_ref, seq_tile, g_ref, wg_ref, wu_ref, cw_ref, cb_ref, wd_ref,
                      hn_ref, act_ref, carry_ref)


def _ab_out_ffn(h, ya, yb, woa, wob, g, wg, wu, cw, cb, wd, seq_len):
    n = h.shape[0]
    tm = ROW_TILE
    row = lambda c: pl.BlockSpec((tm, c), lambda i: (i, 0))
    consts = (woa, wob, g, wg, wu, cw, cb, wd)
    return pl.pallas_call(
        functools.partial(_ab_out_ffn_kernel, tiles_per_seq=seq_len // tm),
        grid=(n // tm,),
        in_specs=[row(D_MODEL), row(ya.shape[1]), row(yb.shape[1])]
                 + [_const_spec(a.shape) for a in consts],
        out_specs=row(D_MODEL),
        out_shape=jax.ShapeDtypeStruct((n, D_MODEL), F32),
        scratch_shapes=_ffn_scratch(tm),
        compiler_params=pltpu.CompilerParams(dimension_semantics=("arbitrary",),
                                             vmem_limit_bytes=VMEM_LIMIT),
        name="ab_out_ffn",
    )(h, ya, yb, *consts)


def _sgu_ffn_kernel(h_ref, cg_ref, win_ref, lng_ref, lnb_ref, ws_ref, bs_ref, wout_ref,
                    g_ref, wg_ref, wu_ref, cw_ref, cb_ref, wd_ref, fg_ref, o_ref,
                    u_ref, v_ref, gated_ref, hn_ref, act_ref, carry_ref, *, tiles_per_seq):
    tm = h_ref.shape[0]
    seq_tile = pl.program_id(0) % tiles_per_seq
    h = h_ref[...]
    xn = _rms(h, cg_ref[...]).astype(BF16)
    u_ref[...] = _gelu(_dot(xn, win_ref[:, :D_MODEL]))
    v = _gelu(_dot(xn, win_ref[:, D_MODEL:]))
    mu = jnp.mean(v, axis=-1, keepdims=True)
    vc = v - mu
    var = jnp.mean(vc * vc, axis=-1, keepdims=True)
    v_ref[...] = (vc * lax.rsqrt(var + NORM_EPS) * lng_ref[...] + lnb_ref[...]).astype(BF16)

    n_chunks = tm // CHUNK
    r = lax.broadcasted_iota(jnp.int32, (CHUNK, CHUNK), 0)
    c = lax.broadcasted_iota(jnp.int32, (CHUNK, CHUNK), 1)
    for gp in range(SGU_GROUPS):
        lanes = slice(gp * CHUNK, (gp + 1) * CHUNK)
        w = jnp.where(c <= r, ws_ref[gp], 0.0).astype(BF16)
        rhs = jnp.concatenate([v_ref[ck * CHUNK:(ck + 1) * CHUNK, lanes] for ck in range(n_chunks)],
                              axis=1)
        sg = _dot(w, rhs) + bs_ref[:, gp:gp + 1]
        for ck in range(n_chunks):
            rows = slice(ck * CHUNK, (ck + 1) * CHUNK)
            gated_ref[rows, lanes] = (u_ref[rows, lanes] * sg[:, ck * CHUNK:(ck + 1) * CHUNK]).astype(BF16)

    o_ref[...] = h_ref[...] + _dot(gated_ref[...], wout_ref[...])
    h2 = _ffn(o_ref, seq_tile, g_ref, wg_ref, wu_ref, cw_ref, cb_ref, wd_ref, hn_ref, act_ref, carry_ref)
    o_ref[...] = _rms(h2, fg_ref[...])


def _sgu_ffn(h, cg, win, lng, lnb, ws, bs_t, wout, g, wg, wu, cw, cb, wd, fg, seq_len):
    n = h.shape[0]
    tm = ROW_TILE
    row = lambda c: pl.BlockSpec((tm, c), lambda i: (i, 0))
    consts = (cg, win, lng, lnb, ws, bs_t, wout, g, wg, wu, cw, cb, wd, fg)
    return pl.pallas_call(
        functools.partial(_sgu_ffn_kernel, tiles_per_seq=seq_len // tm),
        grid=(n // tm,),
        in_specs=[row(D_MODEL)] + [_const_spec(a.shape) for a in consts],
        out_specs=row(D_MODEL),
        out_shape=jax.ShapeDtypeStruct((n, D_MODEL), F32),
        scratch_shapes=[pltpu.VMEM((tm, D_MODEL), F32), pltpu.VMEM((tm, D_MODEL), BF16),
                        pltpu.VMEM((tm, D_MODEL), BF16)] + _ffn_scratch(tm),
        compiler_params=pltpu.CompilerParams(dimension_semantics=("arbitrary",),
                                             vmem_limit_bytes=VMEM_LIMIT),
        name="sgu_ffn",
    )(h, *consts)


def _ffn_params(w_gate, w_up, conv_w, conv_b, w_down):
    return (w_gate.astype(BF16), w_up.astype(BF16), conv_w, conv_b.reshape(1, D_FF),
            w_down.astype(BF16))


def _block_diag(w):
    heads, blk, _ = w.shape
    eye = jnp.eye(heads, dtype=w.dtype)
    return (w[:, :, None, :] * eye[:, None, :, None]).reshape(heads * blk, heads * blk)


def kernel(x, positions, ab_norm, ab_w_in, ab_q_norm, ab_w_q_b, ab_kv_norm, ab_w_kv_b, ab_conv_w, ab_conv_b, ab_w_rg_a, ab_b_rg_a, ab_w_rg_x, ab_b_rg_x, ab_lambda, ab_w_out, c_norm, c_w_in, c_ln_g, c_ln_b, c_w_s, c_b_s, c_w_out, ffn_norm, ffn_w_gate, ffn_w_up, ffn_conv_w, ffn_conv_b, ffn_w_down, final_norm):
    b, s, d = x.shape
    n = b * s
    h = x.reshape(n, d)
    pos = positions.reshape(n, 1)

    w_in = ab_w_in[0]
    o2 = Q_LORA + KV_LORA
    o3 = o2 + QK_ROPE
    zeros = lambda c: jnp.zeros((d, c), w_in.dtype)
    w_in_p = jnp.concatenate([w_in[:, :o2], zeros(QK_NOPE), w_in[:, o2:o3],
                              zeros(HEAD_PAD - QK_NOPE - QK_ROPE), w_in[:, o3:]], axis=1).astype(BF16)
    qk = QK_NOPE + QK_ROPE
    wq = jnp.pad(ab_w_q_b[0].reshape(Q_LORA, MLA_HEADS, qk),
                 ((0, 0), (0, 0), (0, HEAD_PAD - qk))).reshape(Q_LORA, MLA_HEADS * HEAD_PAD).astype(BF16)
    wkv = ab_w_kv_b[0].reshape(KV_LORA, MLA_HEADS, QK_NOPE + V_HEAD)
    wk = jnp.pad(wkv[:, :, :QK_NOPE], ((0, 0), (0, 0), (0, HEAD_PAD - QK_NOPE))
                 ).reshape(KV_LORA, MLA_HEADS * HEAD_PAD).astype(BF16)
    wv = wkv[:, :, QK_NOPE:].reshape(KV_LORA, MLA_HEADS * V_HEAD).astype(BF16)
    half = QK_ROPE // 2
    freq = jnp.exp(-math.log(ROPE_BASE) * jnp.arange(half, dtype=F32) / half)
    invf = jnp.concatenate([jnp.zeros((QK_NOPE,), F32), freq, freq,
                            jnp.zeros((HEAD_PAD - QK_NOPE - QK_ROPE,), F32)]).reshape(1, HEAD_PAD)

    q, k, v, x_lru, gate_lru = _ab_in(
        h, pos, ab_norm[0].reshape(1, d), w_in_p, ab_q_norm[0].reshape(1, Q_LORA), wq,
        ab_kv_norm[0].reshape(1, KV_LORA), wk, wv, invf)

    y_mla = _attention(q.reshape(b, s, -1), k.reshape(b, s, -1), v.reshape(b, s, -1))
    y_lru = _rglru(x_lru.reshape(b, s, LRU_WIDTH), gate_lru.reshape(b, s, LRU_WIDTH),
                   ab_conv_w[0], ab_conv_b[0].reshape(1, -1),
                   _block_diag(ab_w_rg_a[0]).astype(BF16), ab_b_rg_a[0].reshape(1, -1),
                   _block_diag(ab_w_rg_x[0]).astype(BF16), ab_b_rg_x[0].reshape(1, -1),
                   ab_lambda[0].reshape(1, -1))

    w_out = ab_w_out[0].astype(BF16)
    mla_w = MLA_HEADS * V_HEAD
    h = _ab_out_ffn(h, y_mla.reshape(n, mla_w), y_lru.reshape(n, LRU_WIDTH),
                    w_out[:mla_w], w_out[mla_w:], ffn_norm[0].reshape(1, d),
                    *_ffn_params(ffn_w_gate[0], ffn_w_up[0], ffn_conv_w[0], ffn_conv_b[0],
                                 ffn_w_down[0]), seq_len=s)

    out = _sgu_ffn(h, c_norm[0].reshape(1, d), c_w_in[0].astype(BF16), c_ln_g[0].reshape(1, -1),
                   c_ln_b[0].reshape(1, -1), c_w_s[0], c_b_s[0].T, c_w_out[0].astype(BF16),
                   ffn_norm[1].reshape(1, d),
                   *_ffn_params(ffn_w_gate[1], ffn_w_up[1], ffn_conv_w[1], ffn_conv_b[1],
                                ffn_w_down[1]), final_norm.reshape(1, d), seq_len=s)
    return out.reshape(b, s, d)
```

```python
import functools
import math

import jax
import jax.numpy as jnp
from jax import lax
from jax.experimental import pallas as pl
from jax.experimental.pallas import tpu as pltpu

F32 = jnp.float32
BF16 = jnp.bfloat16

D_MODEL = 1024
MLA_HEADS = 8
Q_LORA = 256
KV_LORA = 128
QK_NOPE = 64
QK_ROPE = 32
V_HEAD = 64
ROPE_BASE = 10000.0
LRU_WIDTH = 512
LRU_HEADS = 8
LRU_CONV = 4
LRU_C = 8.0
CHUNK = 128
SGU_GROUPS = 8
D_FF = 2816
FFN_CONV = 3
NORM_EPS = 1e-6

LANES = 128
SUBLANES = 8
HEAD_PAD = 128
AB_IN_PAD = Q_LORA + KV_LORA + HEAD_PAD + 2 * LRU_WIDTH

ROW_TILE = 512
AB_IN_TILE = 1024
ATTN_TILE = 256
LRU_TILE = 256
FF_CHUNK = 256
N_FF_CHUNKS = D_FF // FF_CHUNK
VMEM_LIMIT = 56 * 1024 * 1024


def _dot(a, b):
    return jnp.dot(a, b, preferred_element_type=F32)


def _dot_nt(a, b):
    return lax.dot_general(a, b, (((1,), (1,)), ((), ())), preferred_element_type=F32)


def _gelu(x):
    c = math.sqrt(2.0 / math.pi)
    return 0.5 * x * (1.0 + jnp.tanh(c * (x + 0.044715 * (x * x * x))))


def _sigmoid(x):
    return 0.5 * jnp.tanh(0.5 * x) + 0.5


def _expm1(x):
    u = jnp.exp(x)
    small = jnp.where(u == 1.0, x, (u - 1.0) * x / jnp.log(u))
    return jnp.where(x < -0.5, u - 1.0, small)


def _rms(x, g):
    ms = jnp.mean(x * x, axis=-1, keepdims=True)
    return x * lax.rsqrt(ms + NORM_EPS) * g


def _shift_rows(x, prev8, k):
    rolled = pltpu.roll(x, k, 0)
    row = lax.broadcasted_iota(jnp.int32, (SUBLANES, x.shape[1]), 0)
    top = jnp.where(row < k, pltpu.roll(prev8, k, 0), rolled[:SUBLANES])
    return jnp.concatenate([top, rolled[SUBLANES:]], axis=0)


def _const_spec(shape):
    nd = len(shape)
    return pl.BlockSpec(shape, lambda *_: (0,) * nd, pipeline_mode=pl.Buffered(1))


def _ab_in_kernel(h_ref, pos_ref, g_ref, w_in_ref, qg_ref, wq_ref, kvg_ref, wk_ref, wv_ref,
                  invf_ref, q_ref, k_ref, v_ref, xl_ref, gate_ref, *, scale, sub):
    tm = h_ref.shape[0]
    o1 = Q_LORA
    o2 = o1 + KV_LORA
    o3 = o2 + HEAD_PAD
    o4 = o3 + LRU_WIDTH
    half = QK_ROPE // 2
    x1_lo, x2_lo, x2_hi = QK_NOPE, QK_NOPE + half, QK_NOPE + QK_ROPE

    def rope(blk, c, sd, su):
        return blk * c + pltpu.roll(blk, half, 1) * sd + pltpu.roll(blk, LANES - half, 1) * su

    def norm_stage(r):
        groups = LANES // QK_ROPE
        nb = sub // groups
        lane = lax.broadcasted_iota(jnp.int32, (nb, LANES), 1)
        pos_c = pos_ref[r.start + (groups - 1) * nb:r.start + groups * nb, :]
        for gi in range(groups - 2, -1, -1):
            pos_c = jnp.where(lane < (gi + 1) * QK_ROPE,
                              pos_ref[r.start + gi * nb:r.start + (gi + 1) * nb, :], pos_c)
        ang = pos_c.astype(F32) * invf_ref[...]
        cos_c = jnp.cos(ang)
        sin_c = jnp.sin(ang)
        c_tab, s_dn, s_up = [], [], []
        for gi in range(groups):
            shift = (x1_lo - gi * QK_ROPE) % LANES
            cosv = cos_c if shift == 0 else pltpu.roll(cos_c, shift, 1)
            sinv = sin_c if shift == 0 else pltpu.roll(sin_c, shift, 1)
            c_tab.append(jnp.where(lane < x1_lo, 1.0, jnp.where(lane < x2_hi, cosv, 0.0)))
            s_dn.append(jnp.where((lane >= x2_lo) & (lane < x2_hi), sinv, 0.0))
            s_up.append(jnp.where((lane >= x1_lo) & (lane < x2_lo), -sinv, 0.0))
        tabs = tuple(jnp.concatenate(t, axis=0) for t in (c_tab, s_dn, s_up))
        xn = _rms(h_ref[r, :], g_ref[...]).astype(BF16)
        return r, xn, tabs

    def proj_stage(r, xn, tabs):
        xl_ref[r, :] = _dot(xn, w_in_ref[:, o3:o4])
        gate_ref[r, :] = _dot(xn, w_in_ref[:, o4:])
        c_q = _dot(xn, w_in_ref[:, :o1])
        c_kv = _dot(xn, w_in_ref[:, o1:o2])
        kpe = _dot(xn, w_in_ref[:, o2:o3])
        return r, c_q, c_kv, kpe, tabs

    def latent_stage(r, c_q, c_kv, kpe, tabs):
        qn = _rms(c_q, qg_ref[...]).astype(BF16)
        width = MLA_HEADS * HEAD_PAD
        qf = _dot(qn, wq_ref[:, :width])
        qr = _dot(qn, wq_ref[:, width:])
        kvn = _rms(c_kv, kvg_ref[...]).astype(BF16)
        kf = _dot(kvn, wk_ref[...])
        v_ref[r, :] = _dot(kvn, wv_ref[...]).astype(BF16)
        return r, qf, qr, kf, kpe, tabs

    def rope_stage(r, qf, qr, kf, kpe, tabs):
        c_tab, s_dn, s_up = tabs
        cq, sq = c_tab * scale, (s_dn - s_up) * scale
        kpe_r = rope(kpe, c_tab, s_dn, s_up)
        for hd in range(MLA_HEADS):
            sl = slice(hd * HEAD_PAD, (hd + 1) * HEAD_PAD)
            q_ref[r, sl] = (qf[:, sl] * cq + qr[:, sl] * sq).astype(BF16)
            k_ref[r, sl] = (kf[:, sl] + kpe_r).astype(BF16)

    stages = (proj_stage, latent_stage, rope_stage)
    n_sub = tm // sub
    live = [None] * len(stages)
    for step in range(n_sub + len(stages)):
        nxt = [None] * len(stages)
        if step < n_sub:
            nxt[0] = norm_stage(slice(step * sub, (step + 1) * sub))
        for si, stage in enumerate(stages):
            if live[si] is not None:
                out = stage(*live[si])
                if si + 1 < len(stages):
                    nxt[si + 1] = out
        live = nxt


def _ab_in(h, pos, g, w_in, qg, wq, kvg, wk, wv, invf):
    n = h.shape[0]
    tm = AB_IN_TILE
    row = lambda c: pl.BlockSpec((tm, c), lambda i: (i, 0))
    scale = float((QK_NOPE + QK_ROPE) ** -0.5 * math.log2(math.e))
    return pl.pallas_call(
        functools.partial(_ab_in_kernel, scale=scale, sub=ROW_TILE),
        grid=(n // tm,),
        in_specs=[row(D_MODEL), row(1), _const_spec(g.shape), _const_spec(w_in.shape),
                  _const_spec(qg.shape), _const_spec(wq.shape), _const_spec(kvg.shape),
                  _const_spec(wk.shape), _const_spec(wv.shape), _const_spec(invf.shape)],
        out_specs=[row(MLA_HEADS * HEAD_PAD), row(MLA_HEADS * HEAD_PAD), row(MLA_HEADS * V_HEAD),
                   row(LRU_WIDTH), row(LRU_WIDTH)],
        out_shape=[jax.ShapeDtypeStruct((n, MLA_HEADS * HEAD_PAD), BF16),
                   jax.ShapeDtypeStruct((n, MLA_HEADS * HEAD_PAD), BF16),
                   jax.ShapeDtypeStruct((n, MLA_HEADS * V_HEAD), BF16),
                   jax.ShapeDtypeStruct((n, LRU_WIDTH), F32),
                   jax.ShapeDtypeStruct((n, LRU_WIDTH), F32)],
        compiler_params=pltpu.CompilerParams(dimension_semantics=("arbitrary",),
                                             vmem_limit_bytes=VMEM_LIMIT),
        name="ab_in",
    )(h, pos, g, w_in, qg, wq, kvg, wk, wv, invf)


def _attn_kernel(q_ref, k_ref, v_ref, o_ref):
    s_len = q_ref.shape[0]
    t = ATTN_TILE
    r = lax.broadcasted_iota(jnp.int32, (t, t), 0)
    c = lax.broadcasted_iota(jnp.int32, (t, t), 1)
    causal = c <= r
    lane = lax.broadcasted_iota(jnp.int32, (t, LANES), 1)
    n_tiles = s_len // t

    def scores(i, hd):
        hl = slice(hd * HEAD_PAD, (hd + 1) * HEAD_PAD)
        return _dot_nt(q_ref[i * t:(i + 1) * t, hl], k_ref[:(i + 1) * t, hl])

    def softmax(i, s):
        kv = (i + 1) * t
        diag = jnp.where(causal, s[:, kv - t:], -jnp.inf)
        s = diag if i == 0 else jnp.concatenate([s[:, :kv - t], diag], axis=1)
        p = jnp.exp2(s - jnp.max(s, axis=1, keepdims=True))
        return p.astype(BF16), jnp.sum(p, axis=1, keepdims=True)

    def values(i, p, l):
        return _dot(p, v_ref[:(i + 1) * t, :]) / l

    s_cur = p_cur = None
    for step in range(n_tiles + 2):
        s_next = [scores(step, hd) for hd in range(2)] if step < n_tiles else None
        p_next = [softmax(step - 1, s) for s in s_cur] if s_cur is not None else None
        if p_cur is not None:
            i = step - 2
            o0, o1 = (values(i, p, l) for p, l in p_cur)
            o_ref[i * t:(i + 1) * t, :] = jnp.where(lane < V_HEAD, o0, o1).astype(o_ref.dtype)
        s_cur, p_cur = s_next, p_next


def _attention(q, k, v):
    b, s, _ = q.shape
    pairs = MLA_HEADS // 2
    return pl.pallas_call(
        _attn_kernel,
        grid=(b, pairs),
        in_specs=[pl.BlockSpec((None, s, 2 * HEAD_PAD), lambda bi, p: (bi, 0, p)),
                  pl.BlockSpec((None, s, 2 * HEAD_PAD), lambda bi, p: (bi, 0, p)),
                  pl.BlockSpec((None, s, 2 * V_HEAD), lambda bi, p: (bi, 0, p))],
        out_specs=pl.BlockSpec((None, s, 2 * V_HEAD), lambda bi, p: (bi, 0, p)),
        out_shape=jax.ShapeDtypeStruct((b, s, MLA_HEADS * V_HEAD), BF16),
        compiler_params=pltpu.CompilerParams(dimension_semantics=("arbitrary", "arbitrary"),
                                             vmem_limit_bytes=VMEM_LIMIT),
        name="mla_attn",
    )(q, k, v)


def _rglru_kernel(x_ref, gate_ref, cw_ref, cb_ref, wa_ref, ba_ref, wx_ref, bx_ref, lam_ref,
                  y_ref, xprev_ref, hprev_ref, a_ref, b_ref):
    ts, c = x_ref.shape

    @pl.when(pl.program_id(1) == 0)
    def _():
        xprev_ref[...] = jnp.zeros_like(xprev_ref)
        hprev_ref[...] = jnp.zeros_like(hprev_ref)

    x = x_ref[...]
    prev = xprev_ref[...]
    xprev_ref[...] = x[ts - SUBLANES:]
    xc = cb_ref[...] + cw_ref[LRU_CONV - 1:LRU_CONV, :] * x
    for k in range(1, LRU_CONV):
        xc = xc + cw_ref[LRU_CONV - 1 - k:LRU_CONV - k, :] * _shift_rows(x, prev, k)

    xb = xc.astype(BF16)
    r = _sigmoid(_dot(xb, wa_ref[...]) + ba_ref[...])
    gi = _sigmoid(_dot(xb, wx_ref[...]) + bx_ref[...])
    z = -lam_ref[...]
    softplus = jnp.maximum(z, 0.0) + jnp.log1p(jnp.exp(-jnp.abs(z)))
    log_a = (-LRU_C) * r * softplus
    a_ref[...] = jnp.exp(log_a)
    b_ref[...] = jnp.sqrt(-_expm1(2.0 * log_a)) * (gi * xc)

    row = lax.broadcasted_iota(jnp.int32, (SUBLANES, c), 0)

    def group(g, hprev):
        r0 = pl.multiple_of(g * SUBLANES, SUBLANES)
        a = a_ref[pl.ds(r0, SUBLANES), :]
        b = b_ref[pl.ds(r0, SUBLANES), :]
        for k in (1, 2, 4):
            a_sh = jnp.where(row >= k, pltpu.roll(a, k, 0), 1.0)
            b_sh = jnp.where(row >= k, pltpu.roll(b, k, 0), 0.0)
            b = a * b_sh + b
            a = a * a_sh
        h = a * hprev + b
        b_ref[pl.ds(r0, SUBLANES), :] = h
        return jnp.broadcast_to(h[SUBLANES - 1:, :], (SUBLANES, c))

    hprev_ref[...] = lax.fori_loop(0, ts // SUBLANES, group, hprev_ref[...])
    y_ref[...] = (b_ref[...] * _gelu(gate_ref[...])).astype(y_ref.dtype)


def _rglru(x, gate, cw, cb, wa, ba, wx, bx, lam):
    b, s, c = x.shape
    ts = LRU_TILE
    seq = pl.BlockSpec((None, ts, c), lambda bi, i: (bi, i, 0))
    return pl.pallas_call(
        _rglru_kernel,
        grid=(b, s // ts),
        in_specs=[seq, seq, _const_spec(cw.shape), _const_spec(cb.shape), _const_spec(wa.shape),
                  _const_spec(ba.shape), _const_spec(wx.shape), _const_spec(bx.shape),
                  _const_spec(lam.shape)],
        out_specs=seq,
        out_shape=jax.ShapeDtypeStruct((b, s, c), BF16),
        scratch_shapes=[pltpu.VMEM((SUBLANES, c), F32), pltpu.VMEM((SUBLANES, c), F32),
                        pltpu.VMEM((ts, c), F32), pltpu.VMEM((ts, c), F32)],
        compiler_params=pltpu.CompilerParams(dimension_semantics=("arbitrary", "arbitrary"),
                                             vmem_limit_bytes=VMEM_LIMIT),
        name="rglru",
    )(x, gate, cw, cb, wa, ba, wx, bx, lam)


def _ffn(res_ref, seq_tile, g_ref, wg_ref, wu_ref, cw_ref, cb_ref, wd_ref, hn_ref, act_ref, carry_ref):
    tm = res_ref.shape[0]
    hn_ref[...] = _rms(res_ref[...], g_ref[...]).astype(BF16)

    @pl.when(seq_tile == 0)
    def _():
        carry_ref[...] = jnp.zeros_like(carry_ref)

    for f in range(N_FF_CHUNKS):
        cols = slice(f * FF_CHUNK, (f + 1) * FF_CHUNK)
        hn = hn_ref[...]
        g = _dot(hn, wg_ref[:, cols])
        u = _dot(hn, wu_ref[:, cols])
        prev = carry_ref[:, cols]
        carry_ref[:, cols] = g[tm - SUBLANES:]
        y = cb_ref[:, cols] + cw_ref[FFN_CONV - 1:FFN_CONV, cols] * g
        for k in range(1, FFN_CONV):
            y = y + cw_ref[FFN_CONV - 1 - k:FFN_CONV - k, cols] * _shift_rows(g, prev, k)
        act_ref[:, cols] = (_gelu(y) * u).astype(BF16)
    return res_ref[...] + _dot(act_ref[...], wd_ref[...])


def _ffn_scratch(tm):
    return [pltpu.VMEM((tm, D_MODEL), BF16), pltpu.VMEM((tm, D_FF), BF16),
            pltpu.VMEM((SUBLANES, D_FF), F32)]


def _ab_out_ffn_kernel(h_ref, ya_ref, yb_ref, woa_ref, wob_ref, g_ref, wg_ref, wu_ref, cw_ref,
                       cb_ref, wd_ref, o_ref, hn_ref, act_ref, carry_ref, *, tiles_per_seq):
    seq_tile = pl.program_id(0) % tiles_per_seq
    o_ref[...] = h_ref[...] + _dot(ya_ref[...], woa_ref[...]) + _dot(yb_ref[...], wob_ref[...])
    o_ref[...] = _ffn(o_ref, seq_tile, g_ref, wg_ref, wu_ref, cw_ref, cb_ref, wd_ref,
                      hn_ref, act_ref, carry_ref)


def _ab_out_ffn(h, ya, yb, woa, wob, g, wg, wu, cw, cb, wd, seq_len):
    n = h.shape[0]
    tm = ROW_TILE
    row = lambda c: pl.BlockSpec((tm, c), lambda i: (i, 0))
    consts = (woa, wob, g, wg, wu, cw, cb, wd)
    return pl.pallas_call(
        functools.partial(_ab_out_ffn_kernel, tiles_per_seq=seq_len // tm),
        grid=(n // tm,),
        in_specs=[row(D_MODEL), row(ya.shape[1]), row(yb.shape[1])]
                 + [_const_spec(a.shape) for a in consts],
        out_specs=row(D_MODEL),
        out_shape=jax.ShapeDtypeStruct((n, D_MODEL), F32),
        scratch_shapes=_ffn_scratch(tm),
        compiler_params=pltpu.CompilerParams(dimension_semantics=("arbitrary",),
                                             vmem_limit_bytes=VMEM_LIMIT),
        name="ab_out_ffn",
    )(h, ya, yb, *consts)


def _sgu_ffn_kernel(h_ref, cg_ref, win_ref, lng_ref, lnb_ref, ws_ref, bs_ref, wout_ref,
                    g_ref, wg_ref, wu_ref, cw_ref, cb_ref, wd_ref, fg_ref, o_ref,
                    u_ref, v_ref, gated_ref, hn_ref, act_ref, carry_ref, *, tiles_per_seq):
    tm = h_ref.shape[0]
    seq_tile = pl.program_id(0) % tiles_per_seq
    h = h_ref[...]
    xn = _rms(h, cg_ref[...]).astype(BF16)
    u_ref[...] = _gelu(_dot(xn, win_ref[:, :D_MODEL]))
    v = _gelu(_dot(xn, win_ref[:, D_MODEL:]))
    mu = jnp.mean(v, axis=-1, keepdims=True)
    vc = v - mu
    var = jnp.mean(vc * vc, axis=-1, keepdims=True)
    v_ref[...] = (vc * lax.rsqrt(var + NORM_EPS) * lng_ref[...] + lnb_ref[...]).astype(BF16)

    n_chunks = tm // CHUNK
    r = lax.broadcasted_iota(jnp.int32, (CHUNK, CHUNK), 0)
    c = lax.broadcasted_iota(jnp.int32, (CHUNK, CHUNK), 1)
    for gp in range(SGU_GROUPS):
        lanes = slice(gp * CHUNK, (gp + 1) * CHUNK)
        w = jnp.where(c <= r, ws_ref[gp], 0.0).astype(BF16)
        rhs = jnp.concatenate([v_ref[ck * CHUNK:(ck + 1) * CHUNK, lanes] for ck in range(n_chunks)],
                              axis=1)
        sg = _dot(w, rhs) + bs_ref[:, gp:gp + 1]
        for ck in range(n_chunks):
            rows = slice(ck * CHUNK, (ck + 1) * CHUNK)
            gated_ref[rows, lanes] = (u_ref[rows, lanes] * sg[:, ck * CHUNK:(ck + 1) * CHUNK]).astype(BF16)

    o_ref[...] = h_ref[...] + _dot(gated_ref[...], wout_ref[...])
    h2 = _ffn(o_ref, seq_tile, g_ref, wg_ref, wu_ref, cw_ref, cb_ref, wd_ref, hn_ref, act_ref, carry_ref)
    o_ref[...] = _rms(h2, fg_ref[...])


def _sgu_ffn(h, cg, win, lng, lnb, ws, bs_t, wout, g, wg, wu, cw, cb, wd, fg, seq_len):
    n = h.shape[0]
    tm = ROW_TILE
    row = lambda c: pl.BlockSpec((tm, c), lambda i: (i, 0))
    consts = (cg, win, lng, lnb, ws, bs_t, wout, g, wg, wu, cw, cb, wd, fg)
    return pl.pallas_call(
        functools.partial(_sgu_ffn_kernel, tiles_per_seq=seq_len // tm),
        grid=(n // tm,),
        in_specs=[row(D_MODEL)] + [_const_spec(a.shape) for a in consts],
        out_specs=row(D_MODEL),
        out_shape=jax.ShapeDtypeStruct((n, D_MODEL), F32),
        scratch_shapes=[pltpu.VMEM((tm, D_MODEL), F32), pltpu.VMEM((tm, D_MODEL), BF16),
                        pltpu.VMEM((tm, D_MODEL), BF16)] + _ffn_scratch(tm),
        compiler_params=pltpu.CompilerParams(dimension_semantics=("arbitrary",),
                                             vmem_limit_bytes=VMEM_LIMIT),
        name="sgu_ffn",
    )(h, *consts)


def _ffn_params(w_gate, w_up, conv_w, conv_b, w_down):
    return (w_gate.astype(BF16), w_up.astype(BF16), conv_w, conv_b.reshape(1, D_FF),
            w_down.astype(BF16))


def _block_diag(w):
    heads, blk, _ = w.shape
    eye = jnp.eye(heads, dtype=w.dtype)
    return (w[:, :, None, :] * eye[:, None, :, None]).reshape(heads * blk, heads * blk)


def kernel(x, positions, ab_norm, ab_w_in, ab_q_norm, ab_w_q_b, ab_kv_norm, ab_w_kv_b, ab_conv_w, ab_conv_b, ab_w_rg_a, ab_b_rg_a, ab_w_rg_x, ab_b_rg_x, ab_lambda, ab_w_out, c_norm, c_w_in, c_ln_g, c_ln_b, c_w_s, c_b_s, c_w_out, ffn_norm, ffn_w_gate, ffn_w_up, ffn_conv_w, ffn_conv_b, ffn_w_down, final_norm):
    b, s, d = x.shape
    n = b * s
    h = x.reshape(n, d)
    pos = positions.reshape(n, 1)

    w_in = ab_w_in[0]
    o2 = Q_LORA + KV_LORA
    o3 = o2 + QK_ROPE
    zeros = lambda c: jnp.zeros((d, c), w_in.dtype)
    w_in_p = jnp.concatenate([w_in[:, :o2], zeros(QK_NOPE), w_in[:, o2:o3],
                              zeros(HEAD_PAD - QK_NOPE - QK_ROPE), w_in[:, o3:]], axis=1).astype(BF16)
    qk = QK_NOPE + QK_ROPE
    half = QK_ROPE // 2
    wq3 = ab_w_q_b[0].reshape(Q_LORA, MLA_HEADS, qk)
    pad_q = lambda w: jnp.pad(w, ((0, 0), (0, 0), (0, HEAD_PAD - w.shape[-1]))
                              ).reshape(Q_LORA, MLA_HEADS * HEAD_PAD)
    wq_rot = jnp.concatenate([jnp.zeros_like(wq3[..., :QK_NOPE]), -wq3[..., QK_NOPE + half:],
                              wq3[..., QK_NOPE:QK_NOPE + half]], axis=-1)
    wq = jnp.concatenate([pad_q(wq3), pad_q(wq_rot)], axis=1).astype(BF16)
    wkv = ab_w_kv_b[0].reshape(KV_LORA, MLA_HEADS, QK_NOPE + V_HEAD)
    wk = jnp.pad(wkv[:, :, :QK_NOPE], ((0, 0), (0, 0), (0, HEAD_PAD - QK_NOPE))
                 ).reshape(KV_LORA, MLA_HEADS * HEAD_PAD).astype(BF16)
    wv = wkv[:, :, QK_NOPE:].reshape(KV_LORA, MLA_HEADS * V_HEAD).astype(BF16)
    freq = jnp.exp(-math.log(ROPE_BASE) * jnp.arange(half, dtype=F32) / half)
    invf = jnp.tile(freq, LANES // half).reshape(1, LANES)

    q, k, v, x_lru, gate_lru = _ab_in(
        h, pos, ab_norm[0].reshape(1, d), w_in_p, ab_q_norm[0].reshape(1, Q_LORA), wq,
        ab_kv_norm[0].reshape(1, KV_LORA), wk, wv, invf)

    y_mla = _attention(q.reshape(b, s, -1), k.reshape(b, s, -1), v.reshape(b, s, -1))
    y_lru = _rglru(x_lru.reshape(b, s, LRU_WIDTH), gate_lru.reshape(b, s, LRU_WIDTH),
                   ab_conv_w[0], ab_conv_b[0].reshape(1, -1),
                   _block_diag(ab_w_rg_a[0]).astype(BF16), ab_b_rg_a[0].reshape(1, -1),
                   _block_diag(ab_w_rg_x[0]).astype(BF16), ab_b_rg_x[0].reshape(1, -1),
                   ab_lambda[0].reshape(1, -1))

    w_out = ab_w_out[0].astype(BF16)
    mla_w = MLA_HEADS * V_HEAD
    h = _ab_out_ffn(h, y_mla.reshape(n, mla_w), y_lru.reshape(n, LRU_WIDTH),
                    w_out[:mla_w], w_out[mla_w:], ffn_norm[0].reshape(1, d),
                    *_ffn_params(ffn_w_gate[0], ffn_w_up[0], ffn_conv_w[0], ffn_conv_b[0],
                                 ffn_w_down[0]), seq_len=s)

    out = _sgu_ffn(h, c_norm[0].reshape(1, d), c_w_in[0].astype(BF16), c_ln_g[0].reshape(1, -1),
                   c_ln_b[0].reshape(1, -1), c_w_s[0], c_b_s[0].T, c_w_out[0].astype(BF16),
                   ffn_norm[1].reshape(1, d),
                   *_ffn_params(ffn_w_gate[1], ffn_w_up[1], ffn_conv_w[1], ffn_conv_b[1],
                                ffn_w_down[1]), final_norm.reshape(1, d), seq_len=s)
    return out.reshape(b, s, d)
```

```python
import functools
import math

import jax
import jax.numpy as jnp
from jax import lax
from jax.experimental import pallas as pl
from jax.experimental.pallas import tpu as pltpu

F32 = jnp.float32
BF16 = jnp.bfloat16

D_MODEL = 1024
MLA_HEADS = 8
Q_LORA = 256
KV_LORA = 128
QK_NOPE = 64
QK_ROPE = 32
V_HEAD = 64
ROPE_BASE = 10000.0
LRU_WIDTH = 512
LRU_HEADS = 8
LRU_CONV = 4
LRU_C = 8.0
CHUNK = 128
SGU_GROUPS = 8
D_FF = 2816
FFN_CONV = 3
NORM_EPS = 1e-6

LANES = 128
SUBLANES = 8
HEAD_PAD = 128
AB_IN_PAD = Q_LORA + KV_LORA + HEAD_PAD + 2 * LRU_WIDTH

ROW_TILE = 512
AB_IN_TILE = 1024
ATTN_TILE = 256
LRU_BLOCK_ROWS = 16
LRU_SIDE_PLAN = (3, 3, 3, 3, 3, 3, 2, 2, 2, 2, 2)
FF_CHUNK = 256
N_FF_CHUNKS = D_FF // FF_CHUNK
DOWN_COLS = 256
VMEM_LIMIT = 56 * 1024 * 1024


def _dot(a, b):
    return jnp.dot(a, b, preferred_element_type=F32)


def _dot_nt(a, b):
    return lax.dot_general(a, b, (((1,), (1,)), ((), ())), preferred_element_type=F32)


def _gelu_x2(x):
    c = math.sqrt(2.0 / math.pi)
    t = jnp.tanh(x * (c + (c * 0.044715) * (x * x)))
    return x * t + x


def _gelu(x):
    return 0.5 * _gelu_x2(x)


def _sigmoid(x):
    return 0.5 * jnp.tanh(0.5 * x) + 0.5


def _expm1(x):
    u = jnp.exp(x)
    small = jnp.where(u == 1.0, x, (u - 1.0) * x / jnp.log(u))
    return jnp.where(x < -0.5, u - 1.0, small)


def _rms(x, g):
    ms = jnp.mean(x * x, axis=-1, keepdims=True)
    return x * lax.rsqrt(ms + NORM_EPS) * g


def _shift_rows(x, prev8, k):
    rolled = pltpu.roll(x, k, 0)
    row = lax.broadcasted_iota(jnp.int32, (SUBLANES, x.shape[1]), 0)
    top = jnp.where(row < k, pltpu.roll(prev8, k, 0), rolled[:SUBLANES])
    return jnp.concatenate([top, rolled[SUBLANES:]], axis=0)


def _const_spec(shape):
    nd = len(shape)
    return pl.BlockSpec(shape, lambda *_: (0,) * nd, pipeline_mode=pl.Buffered(1))


def _ab_in_kernel(h_ref, pos_ref, g_ref, w_in_ref, qg_ref, wq_ref, kvg_ref, wk_ref, wv_ref,
                  invf_ref, q_ref, k_ref, v_ref, xl_ref, gate_ref, *, scale, sub):
    tm = h_ref.shape[0]
    o1 = Q_LORA
    o2 = o1 + KV_LORA
    o3 = o2 + HEAD_PAD
    o4 = o3 + LRU_WIDTH
    half = QK_ROPE // 2
    x1_lo, x2_lo, x2_hi = QK_NOPE, QK_NOPE + half, QK_NOPE + QK_ROPE

    def rope(blk, c, sd, su):
        return blk * c + pltpu.roll(blk, half, 1) * sd + pltpu.roll(blk, LANES - half, 1) * su

    def norm_stage(r):
        groups = LANES // QK_ROPE
        nb = sub // groups
        lane = lax.broadcasted_iota(jnp.int32, (nb, LANES), 1)
        pos_c = pos_ref[r.start + (groups - 1) * nb:r.start + groups * nb, :]
        for gi in range(groups - 2, -1, -1):
            pos_c = jnp.where(lane < (gi + 1) * QK_ROPE,
                              pos_ref[r.start + gi * nb:r.start + (gi + 1) * nb, :], pos_c)
        ang = pos_c.astype(F32) * invf_ref[...]
        cos_c = jnp.cos(ang)
        sin_c = jnp.sin(ang)
        c_tab, s_dn, s_up = [], [], []
        for gi in range(groups):
            shift = (x1_lo - gi * QK_ROPE) % LANES
            cosv = cos_c if shift == 0 else pltpu.roll(cos_c, shift, 1)
            sinv = sin_c if shift == 0 else pltpu.roll(sin_c, shift, 1)
            c_tab.append(jnp.where(lane < x1_lo, 1.0, jnp.where(lane < x2_hi, cosv, 0.0)))
            s_dn.append(jnp.where((lane >= x2_lo) & (lane < x2_hi), sinv, 0.0))
            s_up.append(jnp.where((lane >= x1_lo) & (lane < x2_lo), -sinv, 0.0))
        tabs = tuple(jnp.concatenate(t, axis=0) for t in (c_tab, s_dn, s_up))
        xn = _rms(h_ref[r, :], g_ref[...]).astype(BF16)
        return r, xn, tabs

    def proj_stage(r, xn, tabs):
        xl_ref[r, :] = _dot(xn, w_in_ref[:, o3:o4])
        gate_ref[r, :] = _dot(xn, w_in_ref[:, o4:])
        c_q = _dot(xn, w_in_ref[:, :o1])
        c_kv = _dot(xn, w_in_ref[:, o1:o2])
        kpe = _dot(xn, w_in_ref[:, o2:o3])
        return r, c_q, c_kv, kpe, tabs

    def latent_stage(r, c_q, c_kv, kpe, tabs):
        qn = _rms(c_q, qg_ref[...]).astype(BF16)
        width = MLA_HEADS * HEAD_PAD
        qf = _dot(qn, wq_ref[:, :width])
        qr = _dot(qn, wq_ref[:, width:])
        kvn = _rms(c_kv, kvg_ref[...]).astype(BF16)
        kf = _dot(kvn, wk_ref[...])
        v_ref[r, :] = _dot(kvn, wv_ref[...]).astype(BF16)
        return r, qf, qr, kf, kpe, tabs

    def rope_stage(r, qf, qr, kf, kpe, tabs):
        c_tab, s_dn, s_up = tabs
        cq, sq = c_tab * scale, (s_dn - s_up) * scale
        kpe_r = rope(kpe, c_tab, s_dn, s_up)
        for hd in range(MLA_HEADS):
            sl = slice(hd * HEAD_PAD, (hd + 1) * HEAD_PAD)
            q_ref[r, sl] = (qf[:, sl] * cq + qr[:, sl] * sq).astype(BF16)
            k_ref[r, sl] = (kf[:, sl] + kpe_r).astype(BF16)

    stages = (proj_stage, latent_stage, rope_stage)
    n_sub = tm // sub
    live = [None] * len(stages)
    for step in range(n_sub + len(stages)):
        nxt = [None] * len(stages)
        if step < n_sub:
            nxt[0] = norm_stage(slice(step * sub, (step + 1) * sub))
        for si, stage in enumerate(stages):
            if live[si] is not None:
                out = stage(*live[si])
                if si + 1 < len(stages):
                    nxt[si + 1] = out
        live = nxt


def _ab_in(h, pos, g, w_in, qg, wq, kvg, wk, wv, invf):
    n = h.shape[0]
    tm = AB_IN_TILE
    row = lambda c: pl.BlockSpec((tm, c), lambda i: (i, 0))
    scale = float((QK_NOPE + QK_ROPE) ** -0.5 * math.log2(math.e))
    return pl.pallas_call(
        functools.partial(_ab_in_kernel, scale=scale, sub=ROW_TILE),
        grid=(n // tm,),
        in_specs=[row(D_MODEL), row(1), _const_spec(g.shape), _const_spec(w_in.shape),
                  _const_spec(qg.shape), _const_spec(wq.shape), _const_spec(kvg.shape),
                  _const_spec(wk.shape), _const_spec(wv.shape), _const_spec(invf.shape)],
        out_specs=[row(MLA_HEADS * HEAD_PAD), row(MLA_HEADS * HEAD_PAD), row(MLA_HEADS * V_HEAD),
                   row(LRU_WIDTH), row(LRU_WIDTH)],
        out_shape=[jax.ShapeDtypeStruct((n, MLA_HEADS * HEAD_PAD), BF16),
                   jax.ShapeDtypeStruct((n, MLA_HEADS * HEAD_PAD), BF16),
                   jax.ShapeDtypeStruct((n, MLA_HEADS * V_HEAD), BF16),
                   jax.ShapeDtypeStruct((n, LRU_WIDTH), F32),
                   jax.ShapeDtypeStruct((n, LRU_WIDTH), F32)],
        compiler_params=pltpu.CompilerParams(dimension_semantics=("arbitrary",),
                                             vmem_limit_bytes=VMEM_LIMIT),
        name="ab_in",
    )(h, pos, g, w_in, qg, wq, kvg, wk, wv, invf)


def _attn_kernel(q_ref, k_ref, v_ref, o_ref):
    s_len = q_ref.shape[0]
    t = ATTN_TILE
    r = lax.broadcasted_iota(jnp.int32, (t, t), 0)
    c = lax.broadcasted_iota(jnp.int32, (t, t), 1)
    causal = c <= r
    lane = lax.broadcasted_iota(jnp.int32, (t, LANES), 1)
    n_tiles = s_len // t

    def scores(i, hd):
        hl = slice(hd * HEAD_PAD, (hd + 1) * HEAD_PAD)
        return _dot_nt(q_ref[i * t:(i + 1) * t, hl], k_ref[:(i + 1) * t, hl])

    def softmax(i, s):
        kv = (i + 1) * t
        diag = jnp.where(causal, s[:, kv - t:], -jnp.inf)
        s = diag if i == 0 else jnp.concatenate([s[:, :kv - t], diag], axis=1)
        p = jnp.exp2(s - jnp.max(s, axis=1, keepdims=True))
        return p.astype(BF16), jnp.sum(p, axis=1, keepdims=True)

    def values(i, p, l):
        return _dot(p, v_ref[:(i + 1) * t, :]) / l

    s_cur = p_cur = None
    for step in range(n_tiles + 2):
        s_next = [scores(step, hd) for hd in range(2)] if step < n_tiles else None
        p_next = [softmax(step - 1, s) for s in s_cur] if s_cur is not None else None
        if p_cur is not None:
            i = step - 2
            o0, o1 = (values(i, p, l) for p, l in p_cur)
            o_ref[i * t:(i + 1) * t, :] = jnp.where(lane < V_HEAD, o0, o1).astype(o_ref.dtype)
        s_cur, p_cur = s_next, p_next


def _attention(q, k, v):
    b, s, _ = q.shape
    pairs = MLA_HEADS // 2
    return pl.pallas_call(
        _attn_kernel,
        grid=(b, pairs),
        in_specs=[pl.BlockSpec((None, s, 2 * HEAD_PAD), lambda bi, p: (bi, 0, p)),
                  pl.BlockSpec((None, s, 2 * HEAD_PAD), lambda bi, p: (bi, 0, p)),
                  pl.BlockSpec((None, s, 2 * V_HEAD), lambda bi, p: (bi, 0, p))],
        out_specs=pl.BlockSpec((None, s, 2 * V_HEAD), lambda bi, p: (bi, 0, p)),
        out_shape=jax.ShapeDtypeStruct((b, s, MLA_HEADS * V_HEAD), BF16),
        compiler_params=pltpu.CompilerParams(dimension_semantics=("arbitrary", "arbitrary"),
                                             vmem_limit_bytes=VMEM_LIMIT),
        name="mla_attn",
    )(q, k, v)


def _rglru_tile(x_ref, gate_ref, first, lru, y_ref, slot):
    cw_ref, cb_ref, wa_ref, ba_ref, wx_ref, bx_ref, lam_ref, xprev_ref, hprev_ref = lru
    ts, c = x_ref.shape
    x = x_ref[...]
    if first is True:
        prev = jnp.zeros((SUBLANES, c), F32)
        h_in = jnp.zeros((SUBLANES, c), F32)
    else:
        prev = jnp.where(first, 0.0, xprev_ref[...])
        h_in = jnp.where(first, 0.0, hprev_ref[...])
    xprev_ref[...] = x[ts - SUBLANES:]
    xc = cb_ref[...] + cw_ref[LRU_CONV - 1:LRU_CONV, :] * x
    for k in range(1, LRU_CONV):
        xc = xc + cw_ref[LRU_CONV - 1 - k:LRU_CONV - k, :] * _shift_rows(x, prev, k)
    xb = xc.astype(BF16)
    pre_a = _dot(xb, wa_ref[...]) + ba_ref[...]
    pre_x = _dot(xb, wx_ref[...]) + bx_ref[...]
    z = -lam_ref[...]
    softplus = jnp.maximum(z, 0.0) + jnp.log1p(jnp.exp(-jnp.abs(z)))
    row = lax.broadcasted_iota(jnp.int32, (SUBLANES, c), 0)
    yield

    for r0 in range(0, ts, LRU_BLOCK_ROWS):
        rows = slice(r0, r0 + LRU_BLOCK_ROWS)
        log_a = (-LRU_C) * _sigmoid(pre_a[rows]) * softplus
        a_blk = jnp.exp(log_a)
        b_blk = jnp.sqrt(-_expm1(2.0 * log_a)) * (_sigmoid(pre_x[rows]) * xc[rows])
        hs = []
        for g0 in range(0, LRU_BLOCK_ROWS, SUBLANES):
            a = a_blk[g0:g0 + SUBLANES]
            b = b_blk[g0:g0 + SUBLANES]
            for k in (1, 2, 4):
                a_sh = jnp.where(row >= k, pltpu.roll(a, k, 0), 1.0)
                b_sh = jnp.where(row >= k, pltpu.roll(b, k, 0), 0.0)
                b = a * b_sh + b
                a = a * a_sh
            h = a * h_in + b
            hs.append(h)
            h_in = jnp.broadcast_to(h[SUBLANES - 1:, :], (SUBLANES, c))
        y = jnp.concatenate(hs, axis=0) * _gelu(gate_ref[rows, :])
        y_ref[slot, rows, :] = y.astype(y_ref.dtype)
        yield
    hprev_ref[...] = h_in


def _ffn(res_ref, seq_tile, g_ref, wg_ref, wu_ref, cw_ref, cb_ref, wd_ref, hn_ref, act_ref, carry_ref,
         side_work=None, side_plan=None):
    tm = res_ref.shape[0]
    hn_ref[...] = _rms(res_ref[...], g_ref[...]).astype(BF16)

    @pl.when(seq_tile == 0)
    def _():
        carry_ref[...] = jnp.zeros_like(carry_ref)

    for f in range(N_FF_CHUNKS):
        cols = slice(f * FF_CHUNK, (f + 1) * FF_CHUNK)
        hn = hn_ref[...]
        g = _dot(hn, wg_ref[:, cols])
        u = _dot(hn, wu_ref[:, cols])
        prev = carry_ref[:, cols]
        carry_ref[:, cols] = g[tm - SUBLANES:]
        y = cb_ref[:, cols] + cw_ref[FFN_CONV - 1:FFN_CONV, cols] * g
        for k in range(1, FFN_CONV):
            y = y + cw_ref[FFN_CONV - 1 - k:FFN_CONV - k, cols] * _shift_rows(g, prev, k)
        act_ref[:, cols] = (_gelu_x2(y) * u).astype(BF16)
        if side_work is not None:
            for _ in range(side_plan[f]):
                next(side_work, None)
    for c0 in range(0, D_MODEL, DOWN_COLS):
        cols = slice(c0, c0 + DOWN_COLS)
        res_ref[:, cols] = res_ref[:, cols] + _dot(act_ref[...], wd_ref[:, cols])
        if side_work is not None:
            next(side_work, None)
    if side_work is not None:
        for _ in side_work:
            pass


def _ffn_scratch(tm):
    return [pltpu.VMEM((tm, D_MODEL), BF16), pltpu.VMEM((tm, D_FF), BF16),
            pltpu.VMEM((SUBLANES, D_FF), F32)]


def _ab_out_ffn_kernel(h_ref, ya_ref, x0_ref, gate0_ref, xn_ref, gaten_ref,
                       lcw_ref, lcb_ref, wa_ref, ba_ref, wx_ref, bx_ref, lam_ref,
                       woa_ref, wob_ref, g_ref, wg_ref, wu_ref, cw_ref, cb_ref, wd_ref, o_ref,
                       ylru_ref, xprev_ref, hprev_ref, hn_ref, act_ref, carry_ref, *, tiles_per_seq):
    i = pl.program_id(0)
    seq_tile = i % tiles_per_seq
    slot = i % 2
    lru = (lcw_ref, lcb_ref, wa_ref, ba_ref, wx_ref, bx_ref, lam_ref, xprev_ref, hprev_ref)

    @pl.when(i == 0)
    def _():
        for _ in _rglru_tile(x0_ref, gate0_ref, True, lru, ylru_ref, 0):
            pass

    next_starts_seq = (i + 1) % tiles_per_seq == 0
    side = _rglru_tile(xn_ref, gaten_ref, next_starts_seq, lru, ylru_ref, 1 - slot)
    next(side)
    o_ref[...] = h_ref[...] + _dot(ya_ref[...], woa_ref[...]) + _dot(ylru_ref[slot], wob_ref[...])
    _ffn(o_ref, seq_tile, g_ref, wg_ref, wu_ref, cw_ref, cb_ref, wd_ref,
         hn_ref, act_ref, carry_ref, side_work=side, side_plan=LRU_SIDE_PLAN)


def _ab_out_ffn(h, ya, x_lru, gate_lru, lru_consts, woa, wob, g, wg, wu, cw, cb, wd, seq_len):
    n = h.shape[0]
    tm = ROW_TILE
    n_tiles = n // tm
    row = lambda c: pl.BlockSpec((tm, c), lambda i: (i, 0))
    first = pl.BlockSpec((tm, LRU_WIDTH), lambda i: (0, 0), pipeline_mode=pl.Buffered(1))
    ahead = pl.BlockSpec((tm, LRU_WIDTH), lambda i: (jnp.minimum(i + 1, n_tiles - 1), 0))
    consts = (*lru_consts, woa, wob, g, wg, wu, cw, cb, wd)
    return pl.pallas_call(
        functools.partial(_ab_out_ffn_kernel, tiles_per_seq=seq_len // tm),
        grid=(n_tiles,),
        in_specs=[row(D_MODEL), row(ya.shape[1]), first, first, ahead, ahead]
                 + [_const_spec(a.shape) for a in consts],
        out_specs=row(D_MODEL),
        out_shape=jax.ShapeDtypeStruct((n, D_MODEL), F32),
        scratch_shapes=[pltpu.VMEM((2, tm, LRU_WIDTH), BF16), pltpu.VMEM((SUBLANES, LRU_WIDTH), F32),
                        pltpu.VMEM((SUBLANES, LRU_WIDTH), F32)] + _ffn_scratch(tm),
        compiler_params=pltpu.CompilerParams(dimension_semantics=("arbitrary",),
                                             vmem_limit_bytes=VMEM_LIMIT),
        name="ab_out_ffn",
    )(h, ya, x_lru, gate_lru, x_lru, gate_lru, *consts)


def _sgu_ffn_kernel(h_ref, cg_ref, win_ref, lng_ref, lnb_ref, ws_ref, bs_ref, wout_ref,
                    g_ref, wg_ref, wu_ref, cw_ref, cb_ref, wd_ref, fg_ref, o_ref,
                    u_ref, v_ref, gated_ref, hn_ref, act_ref, carry_ref, *, tiles_per_seq):
    tm = h_ref.shape[0]
    seq_tile = pl.program_id(0) % tiles_per_seq
    h = h_ref[...]
    xn = _rms(h, cg_ref[...]).astype(BF16)
    u_ref[...] = _gelu(_dot(xn, win_ref[:, :D_MODEL]))
    v = _gelu(_dot(xn, win_ref[:, D_MODEL:]))
    mu = jnp.mean(v, axis=-1, keepdims=True)
    vc = v - mu
    var = jnp.mean(vc * vc, axis=-1, keepdims=True)
    v_ref[...] = (vc * lax.rsqrt(var + NORM_EPS) * lng_ref[...] + lnb_ref[...]).astype(BF16)

    n_chunks = tm // CHUNK
    r = lax.broadcasted_iota(jnp.int32, (CHUNK, CHUNK), 0)
    c = lax.broadcasted_iota(jnp.int32, (CHUNK, CHUNK), 1)
    for gp in range(SGU_GROUPS):
        lanes = slice(gp * CHUNK, (gp + 1) * CHUNK)
        w = jnp.where(c <= r, ws_ref[gp], 0.0).astype(BF16)
        rhs = jnp.concatenate([v_ref[ck * CHUNK:(ck + 1) * CHUNK, lanes] for ck in range(n_chunks)],
                              axis=1)
        sg = _dot(w, rhs) + bs_ref[:, gp:gp + 1]
        for ck in range(n_chunks):
            rows = slice(ck * CHUNK, (ck + 1) * CHUNK)
            gated_ref[rows, lanes] = (u_ref[rows, lanes] * sg[:, ck * CHUNK:(ck + 1) * CHUNK]).astype(BF16)

    o_ref[...] = h_ref[...] + _dot(gated_ref[...], wout_ref[...])
    _ffn(o_ref, seq_tile, g_ref, wg_ref, wu_ref, cw_ref, cb_ref, wd_ref, hn_ref, act_ref, carry_ref)
    o_ref[...] = _rms(o_ref[...], fg_ref[...])


def _sgu_ffn(h, cg, win, lng, lnb, ws, bs_t, wout, g, wg, wu, cw, cb, wd, fg, seq_len):
    n = h.shape[0]
    tm = ROW_TILE
    row = lambda c: pl.BlockSpec((tm, c), lambda i: (i, 0))
    consts = (cg, win, lng, lnb, ws, bs_t, wout, g, wg, wu, cw, cb, wd, fg)
    return pl.pallas_call(
        functools.partial(_sgu_ffn_kernel, tiles_per_seq=seq_len // tm),
        grid=(n // tm,),
        in_specs=[row(D_MODEL)] + [_const_spec(a.shape) for a in consts],
        out_specs=row(D_MODEL),
        out_shape=jax.ShapeDtypeStruct((n, D_MODEL), F32),
        scratch_shapes=[pltpu.VMEM((tm, D_MODEL), F32), pltpu.VMEM((tm, D_MODEL), BF16),
                        pltpu.VMEM((tm, D_MODEL), BF16)] + _ffn_scratch(tm),
        compiler_params=pltpu.CompilerParams(dimension_semantics=("arbitrary",),
                                             vmem_limit_bytes=VMEM_LIMIT),
        name="sgu_ffn",
    )(h, *consts)


def _ffn_params(w_gate, w_up, conv_w, conv_b, w_down):
    return (w_gate.astype(BF16), (0.5 * w_up).astype(BF16), conv_w, conv_b.reshape(1, D_FF),
            w_down.astype(BF16))


def _block_diag(w):
    heads, blk, _ = w.shape
    eye = jnp.eye(heads, dtype=w.dtype)
    return (w[:, :, None, :] * eye[:, None, :, None]).reshape(heads * blk, heads * blk)


def kernel(x, positions, ab_norm, ab_w_in, ab_q_norm, ab_w_q_b, ab_kv_norm, ab_w_kv_b, ab_conv_w, ab_conv_b, ab_w_rg_a, ab_b_rg_a, ab_w_rg_x, ab_b_rg_x, ab_lambda, ab_w_out, c_norm, c_w_in, c_ln_g, c_ln_b, c_w_s, c_b_s, c_w_out, ffn_norm, ffn_w_gate, ffn_w_up, ffn_conv_w, ffn_conv_b, ffn_w_down, final_norm):
    b, s, d = x.shape
    n = b * s
    h = x.reshape(n, d)
    pos = positions.reshape(n, 1)

    w_in = ab_w_in[0]
    o2 = Q_LORA + KV_LORA
    o3 = o2 + QK_ROPE
    zeros = lambda c: jnp.zeros((d, c), w_in.dtype)
    w_in_p = jnp.concatenate([w_in[:, :o2], zeros(QK_NOPE), w_in[:, o2:o3],
                              zeros(HEAD_PAD - QK_NOPE - QK_ROPE), w_in[:, o3:]], axis=1).astype(BF16)
    qk = QK_NOPE + QK_ROPE
    half = QK_ROPE // 2
    wq3 = ab_w_q_b[0].reshape(Q_LORA, MLA_HEADS, qk)
    pad_q = lambda w: jnp.pad(w, ((0, 0), (0, 0), (0, HEAD_PAD - w.shape[-1]))
                              ).reshape(Q_LORA, MLA_HEADS * HEAD_PAD)
    wq_rot = jnp.concatenate([jnp.zeros_like(wq3[..., :QK_NOPE]), -wq3[..., QK_NOPE + half:],
                              wq3[..., QK_NOPE:QK_NOPE + half]], axis=-1)
    wq = jnp.concatenate([pad_q(wq3), pad_q(wq_rot)], axis=1).astype(BF16)
    wkv = ab_w_kv_b[0].reshape(KV_LORA, MLA_HEADS, QK_NOPE + V_HEAD)
    wk = jnp.pad(wkv[:, :, :QK_NOPE], ((0, 0), (0, 0), (0, HEAD_PAD - QK_NOPE))
                 ).reshape(KV_LORA, MLA_HEADS * HEAD_PAD).astype(BF16)
    wv = wkv[:, :, QK_NOPE:].reshape(KV_LORA, MLA_HEADS * V_HEAD).astype(BF16)
    freq = jnp.exp(-math.log(ROPE_BASE) * jnp.arange(half, dtype=F32) / half)
    invf = jnp.tile(freq, LANES // half).reshape(1, LANES)

    q, k, v, x_lru, gate_lru = _ab_in(
        h, pos, ab_norm[0].reshape(1, d), w_in_p, ab_q_norm[0].reshape(1, Q_LORA), wq,
        ab_kv_norm[0].reshape(1, KV_LORA), wk, wv, invf)

    y_mla = _attention(q.reshape(b, s, -1), k.reshape(b, s, -1), v.reshape(b, s, -1))
    lru_consts = (ab_conv_w[0], ab_conv_b[0].reshape(1, -1),
                  _block_diag(ab_w_rg_a[0]).astype(BF16), ab_b_rg_a[0].reshape(1, -1),
                  _block_diag(ab_w_rg_x[0]).astype(BF16), ab_b_rg_x[0].reshape(1, -1),
                  ab_lambda[0].reshape(1, -1))

    w_out = ab_w_out[0].astype(BF16)
    mla_w = MLA_HEADS * V_HEAD
    h = _ab_out_ffn(h, y_mla.reshape(n, mla_w), x_lru, gate_lru, lru_consts,
                    w_out[:mla_w], w_out[mla_w:], ffn_norm[0].reshape(1, d),
                    *_ffn_params(ffn_w_gate[0], ffn_w_up[0], ffn_conv_w[0], ffn_conv_b[0],
                                 ffn_w_down[0]), seq_len=s)

    out = _sgu_ffn(h, c_norm[0].reshape(1, d), c_w_in[0].astype(BF16), c_ln_g[0].reshape(1, -1),
                   c_ln_b[0].reshape(1, -1), c_w_s[0], c_b_s[0].T, c_w_out[0].astype(BF16),
                   ffn_norm[1].reshape(1, d),
                   *_ffn_params(ffn_w_gate[1], ffn_w_up[1], ffn_conv_w[1], ffn_conv_b[1],
                                ffn_w_down[1]), final_norm.reshape(1, d), seq_len=s)
    return out.reshape(b, s, d)
```

```python
import functools
import math

import jax
import jax.numpy as jnp
from jax import lax
from jax.experimental import pallas as pl
from jax.experimental.pallas import tpu as pltpu

F32 = jnp.float32
BF16 = jnp.bfloat16

D_MODEL = 1024
MLA_HEADS = 8
Q_LORA = 256
KV_LORA = 128
QK_NOPE = 64
QK_ROPE = 32
V_HEAD = 64
ROPE_BASE = 10000.0
LRU_WIDTH = 512
LRU_HEADS = 8
LRU_CONV = 4
LRU_C = 8.0
CHUNK = 128
SGU_GROUPS = 8
D_FF = 2816
FFN_CONV = 3
NORM_EPS = 1e-6

LANES = 128
SUBLANES = 8
HEAD_PAD = 128
AB_IN_PAD = Q_LORA + KV_LORA + HEAD_PAD + 2 * LRU_WIDTH

ROW_TILE = 512
AB_IN_TILE = 1024
ATTN_TILE = 256
LRU_SIDE_PLAN = (2,) * 11 + (3, 3, 2, 2)
FF_CHUNK = 256
N_FF_CHUNKS = D_FF // FF_CHUNK
DOWN_COLS = 256
VMEM_LIMIT = 56 * 1024 * 1024


def _dot(a, b):
    return jnp.dot(a, b, preferred_element_type=F32)


def _dot_nt(a, b):
    return lax.dot_general(a, b, (((1,), (1,)), ((), ())), preferred_element_type=F32)


def _gelu_x2(x):
    c = math.sqrt(2.0 / math.pi)
    t = jnp.tanh(x * (c + (c * 0.044715) * (x * x)))
    return x * t + x


def _gelu(x):
    return 0.5 * _gelu_x2(x)


def _sigmoid(x):
    return 0.5 * jnp.tanh(0.5 * x) + 0.5


def _expm1(x):
    u = jnp.exp(x)
    small = jnp.where(u == 1.0, x, (u - 1.0) * x / jnp.log(u))
    return jnp.where(x < -0.5, u - 1.0, small)


def _rms(x, g):
    ms = jnp.mean(x * x, axis=-1, keepdims=True)
    return x * lax.rsqrt(ms + NORM_EPS) * g


def _shift_rows(x, prev8, k):
    rolled = pltpu.roll(x, k, 0)
    row = lax.broadcasted_iota(jnp.int32, (SUBLANES, x.shape[1]), 0)
    top = jnp.where(row < k, pltpu.roll(prev8, k, 0), rolled[:SUBLANES])
    return jnp.concatenate([top, rolled[SUBLANES:]], axis=0)


def _const_spec(shape):
    nd = len(shape)
    return pl.BlockSpec(shape, lambda *_: (0,) * nd, pipeline_mode=pl.Buffered(1))


def _layer_spec(shape, layer):
    nd = len(shape)
    return pl.BlockSpec((None,) + tuple(shape[1:]), lambda *_: (layer,) + (0,) * (nd - 1),
                        pipeline_mode=pl.Buffered(1))


def _ab_in_kernel(h_ref, pos_ref, g_ref, w_in_ref, qg_ref, wq_ref, kvg_ref, wk_ref, wv_ref,
                  invf_ref, q_ref, k_ref, v_ref, xl_ref, gate_ref, *, scale, sub):
    tm = h_ref.shape[0]
    o1 = Q_LORA
    o2 = o1 + KV_LORA
    o3 = o2 + HEAD_PAD
    o4 = o3 + LRU_WIDTH
    half = QK_ROPE // 2
    x1_lo, x2_lo, x2_hi = QK_NOPE, QK_NOPE + half, QK_NOPE + QK_ROPE

    def rope(blk, c, sd, su):
        return blk * c + pltpu.roll(blk, half, 1) * sd + pltpu.roll(blk, LANES - half, 1) * su

    def norm_stage(r):
        groups = LANES // QK_ROPE
        nb = sub // groups
        lane = lax.broadcasted_iota(jnp.int32, (nb, LANES), 1)
        pos_c = pos_ref[r.start + (groups - 1) * nb:r.start + groups * nb, :]
        for gi in range(groups - 2, -1, -1):
            pos_c = jnp.where(lane < (gi + 1) * QK_ROPE,
                              pos_ref[r.start + gi * nb:r.start + (gi + 1) * nb, :], pos_c)
        ang = pos_c.astype(F32) * invf_ref[...]
        cos_c = jnp.cos(ang)
        sin_c = jnp.sin(ang)
        c_tab, s_dn, s_up = [], [], []
        for gi in range(groups):
            shift = (x1_lo - gi * QK_ROPE) % LANES
            cosv = cos_c if shift == 0 else pltpu.roll(cos_c, shift, 1)
            sinv = sin_c if shift == 0 else pltpu.roll(sin_c, shift, 1)
            c_tab.append(jnp.where(lane < x1_lo, 1.0, jnp.where(lane < x2_hi, cosv, 0.0)))
            s_dn.append(jnp.where((lane >= x2_lo) & (lane < x2_hi), sinv, 0.0))
            s_up.append(jnp.where((lane >= x1_lo) & (lane < x2_lo), -sinv, 0.0))
        tabs = tuple(jnp.concatenate(t, axis=0) for t in (c_tab, s_dn, s_up))
        xn = _rms(h_ref[r, :], g_ref[...]).astype(BF16)
        return r, xn, tabs

    def proj_stage(r, xn, tabs):
        xl_ref[r, :] = _dot(xn, w_in_ref[:, o3:o4])
        gate_ref[r, :] = _dot(xn, w_in_ref[:, o4:])
        c_q = _dot(xn, w_in_ref[:, :o1])
        c_kv = _dot(xn, w_in_ref[:, o1:o2])
        kpe = _dot(xn, w_in_ref[:, o2:o3])
        return r, c_q, c_kv, kpe, tabs

    def latent_stage(r, c_q, c_kv, kpe, tabs):
        qn = _rms(c_q, qg_ref[...]).astype(BF16)
        width = MLA_HEADS * HEAD_PAD
        qf = _dot(qn, wq_ref[:, :width])
        qr = _dot(qn, wq_ref[:, width:])
        kvn = _rms(c_kv, kvg_ref[...]).astype(BF16)
        kf = _dot(kvn, wk_ref[...])
        v_ref[r, :] = _dot(kvn, wv_ref[...]).astype(BF16)
        return r, qf, qr, kf, kpe, tabs

    def rope_stage(r, qf, qr, kf, kpe, tabs):
        c_tab, s_dn, s_up = tabs
        cq, sq = c_tab * scale, (s_dn - s_up) * scale
        kpe_r = rope(kpe, c_tab, s_dn, s_up)
        for hd in range(MLA_HEADS):
            sl = slice(hd * HEAD_PAD, (hd + 1) * HEAD_PAD)
            q_ref[r, sl] = (qf[:, sl] * cq + qr[:, sl] * sq).astype(BF16)
            k_ref[r, sl] = (kf[:, sl] + kpe_r).astype(BF16)

    stages = (proj_stage, latent_stage, rope_stage)
    n_sub = tm // sub
    live = [None] * len(stages)
    for step in range(n_sub + len(stages)):
        nxt = [None] * len(stages)
        if step < n_sub:
            nxt[0] = norm_stage(slice(step * sub, (step + 1) * sub))
        for si, stage in enumerate(stages):
            if live[si] is not None:
                out = stage(*live[si])
                if si + 1 < len(stages):
                    nxt[si + 1] = out
        live = nxt


def _ab_in(h, pos, g, w_in, qg, wq, kvg, wk, wv, invf):
    n = h.shape[0]
    tm = AB_IN_TILE
    row = lambda c: pl.BlockSpec((tm, c), lambda i: (i, 0))
    scale = float((QK_NOPE + QK_ROPE) ** -0.5 * math.log2(math.e))
    return pl.pallas_call(
        functools.partial(_ab_in_kernel, scale=scale, sub=ROW_TILE),
        grid=(n // tm,),
        in_specs=[row(D_MODEL), row(1), _const_spec(g.shape), _const_spec(w_in.shape),
                  _const_spec(qg.shape), _const_spec(wq.shape), _const_spec(kvg.shape),
                  _const_spec(wk.shape), _const_spec(wv.shape), _const_spec(invf.shape)],
        out_specs=[row(MLA_HEADS * HEAD_PAD), row(MLA_HEADS * HEAD_PAD), row(MLA_HEADS * V_HEAD),
                   row(LRU_WIDTH), row(LRU_WIDTH)],
        out_shape=[jax.ShapeDtypeStruct((n, MLA_HEADS * HEAD_PAD), BF16),
                   jax.ShapeDtypeStruct((n, MLA_HEADS * HEAD_PAD), BF16),
                   jax.ShapeDtypeStruct((n, MLA_HEADS * V_HEAD), BF16),
                   jax.ShapeDtypeStruct((n, LRU_WIDTH), F32),
                   jax.ShapeDtypeStruct((n, LRU_WIDTH), F32)],
        compiler_params=pltpu.CompilerParams(dimension_semantics=("arbitrary",),
                                             vmem_limit_bytes=VMEM_LIMIT),
        name="ab_in",
    )(h, pos, g, w_in, qg, wq, kvg, wk, wv, invf)


def _attn_kernel(q_ref, k_ref, v_ref, o_ref):
    s_len = q_ref.shape[0]
    t = ATTN_TILE
    r = lax.broadcasted_iota(jnp.int32, (t, t), 0)
    c = lax.broadcasted_iota(jnp.int32, (t, t), 1)
    causal = c <= r
    lane = lax.broadcasted_iota(jnp.int32, (t, LANES), 1)
    n_tiles = s_len // t

    def scores(i, hd):
        hl = slice(hd * HEAD_PAD, (hd + 1) * HEAD_PAD)
        return _dot_nt(q_ref[i * t:(i + 1) * t, hl], k_ref[:(i + 1) * t, hl])

    def softmax(i, s):
        kv = (i + 1) * t
        diag = jnp.where(causal, s[:, kv - t:], -jnp.inf)
        s = diag if i == 0 else jnp.concatenate([s[:, :kv - t], diag], axis=1)
        p = jnp.exp2(s - jnp.max(s, axis=1, keepdims=True))
        return p.astype(BF16), jnp.sum(p, axis=1, keepdims=True)

    def values(i, p, l):
        return _dot(p, v_ref[:(i + 1) * t, :]) / l

    s_cur = p_cur = None
    for step in range(n_tiles + 2):
        s_next = [scores(step, hd) for hd in range(2)] if step < n_tiles else None
        p_next = [softmax(step - 1, s) for s in s_cur] if s_cur is not None else None
        if p_cur is not None:
            i = step - 2
            o0, o1 = (values(i, p, l) for p, l in p_cur)
            o_ref[i * t:(i + 1) * t, :] = jnp.where(lane < V_HEAD, o0, o1).astype(o_ref.dtype)
        s_cur, p_cur = s_next, p_next


def _attention(q, k, v):
    b, s, _ = q.shape
    pairs = MLA_HEADS // 2
    return pl.pallas_call(
        _attn_kernel,
        grid=(b, pairs),
        in_specs=[pl.BlockSpec((None, s, 2 * HEAD_PAD), lambda bi, p: (bi, 0, p)),
                  pl.BlockSpec((None, s, 2 * HEAD_PAD), lambda bi, p: (bi, 0, p)),
                  pl.BlockSpec((None, s, 2 * V_HEAD), lambda bi, p: (bi, 0, p))],
        out_specs=pl.BlockSpec((None, s, 2 * V_HEAD), lambda bi, p: (bi, 0, p)),
        out_shape=jax.ShapeDtypeStruct((b, s, MLA_HEADS * V_HEAD), BF16),
        compiler_params=pltpu.CompilerParams(dimension_semantics=("arbitrary", "arbitrary"),
                                             vmem_limit_bytes=VMEM_LIMIT),
        name="mla_attn",
    )(q, k, v)


def _rglru_tile(x_ref, gate_ref, first, lru, y_ref, slot):
    (cw_ref, cb_ref, wa_ref, ba_ref, wx_ref, bx_ref, lam_ref,
     xprev_ref, hprev_ref, a_ref, b_ref, h_ref) = lru
    ts, c = x_ref.shape
    slabs = c // LANES
    x = x_ref[...]
    if first is True:
        prev = jnp.zeros((SUBLANES, c), F32)
        h_in = [jnp.zeros((SUBLANES, LANES), F32)] * slabs
    else:
        prev = jnp.where(first, 0.0, xprev_ref[...])
        h_in = [jnp.where(first, 0.0, hprev_ref[:, j * LANES:(j + 1) * LANES]) for j in range(slabs)]
    xprev_ref[...] = x[ts - SUBLANES:]
    xc = cb_ref[...] + cw_ref[LRU_CONV - 1:LRU_CONV, :] * x
    for k in range(1, LRU_CONV):
        xc = xc + cw_ref[LRU_CONV - 1 - k:LRU_CONV - k, :] * _shift_rows(x, prev, k)
    xb = xc.astype(BF16)
    pre_a = _dot(xb, wa_ref[...]) + ba_ref[...]
    pre_x = _dot(xb, wx_ref[...]) + bx_ref[...]
    z = -lam_ref[...]
    softplus = jnp.maximum(z, 0.0) + jnp.log1p(jnp.exp(-jnp.abs(z)))
    row = lax.broadcasted_iota(jnp.int32, (SUBLANES, LANES), 0)
    yield

    blk = SUBLANES * SUBLANES
    for r0 in range(0, ts, blk):
        rows = slice(r0, r0 + blk)
        for j in range(slabs):
            lanes = slice(j * LANES, (j + 1) * LANES)
            log_a = (-LRU_C) * _sigmoid(pre_a[rows, lanes]) * softplus[:, lanes]
            a_ref[j, rows, :] = jnp.exp(log_a)
            b_ref[j, rows, :] = (jnp.sqrt(-_expm1(2.0 * log_a))
                                 * (_sigmoid(pre_x[rows, lanes]) * xc[rows, lanes]))
            hs, ps = [], []
            for r in range(SUBLANES):
                a = a_ref[j, pl.ds(r0 + r, SUBLANES, stride=SUBLANES), :]
                b = b_ref[j, pl.ds(r0 + r, SUBLANES, stride=SUBLANES), :]
                hs.append(b if r == 0 else a * hs[-1] + b)
                ps.append(a if r == 0 else a * ps[-1])
            p, q = ps[-1], hs[-1]
            for k in (1, 2, 4):
                p_sh = jnp.where(row >= k, pltpu.roll(p, k, 0), 1.0)
                q_sh = jnp.where(row >= k, pltpu.roll(q, k, 0), 0.0)
                q = p * q_sh + q
                p = p * p_sh
            seg_out = p * h_in[j] + q
            seg_in = jnp.where(row == 0, h_in[j], pltpu.roll(seg_out, 1, 0))
            for r in range(SUBLANES):
                h_ref[j, pl.ds(r0 + r, SUBLANES, stride=SUBLANES), :] = ps[r] * seg_in + hs[r]
            h_in[j] = jnp.broadcast_to(seg_out[SUBLANES - 1:, :], (SUBLANES, LANES))
            y = h_ref[j, rows, :] * _gelu_x2(gate_ref[rows, lanes])
            y_ref[slot, rows, lanes] = y.astype(y_ref.dtype)
            yield
    hprev_ref[...] = jnp.concatenate(h_in, axis=1)


def _ffn(res_ref, seq_tile, g_ref, wg_ref, wu_ref, cw_ref, cb_ref, wd_ref, hn_ref, act_ref, carry_ref,
         side_work=None, side_plan=None):
    tm = res_ref.shape[0]
    hn_ref[...] = _rms(res_ref[...], g_ref[...]).astype(BF16)

    @pl.when(seq_tile == 0)
    def _():
        carry_ref[...] = jnp.zeros_like(carry_ref)

    for f in range(N_FF_CHUNKS):
        cols = slice(f * FF_CHUNK, (f + 1) * FF_CHUNK)
        hn = hn_ref[...]
        g = _dot(hn, wg_ref[:, cols])
        u = _dot(hn, wu_ref[:, cols])
        prev = carry_ref[:, cols]
        carry_ref[:, cols] = g[tm - SUBLANES:]
        y = cb_ref[:, cols] + cw_ref[FFN_CONV - 1:FFN_CONV, cols] * g
        for k in range(1, FFN_CONV):
            y = y + cw_ref[FFN_CONV - 1 - k:FFN_CONV - k, cols] * _shift_rows(g, prev, k)
        act_ref[:, cols] = (_gelu_x2(y) * u).astype(BF16)
        if side_work is not None:
            for _ in range(side_plan[f]):
                next(side_work, None)
    for c0 in range(0, D_MODEL, DOWN_COLS):
        cols = slice(c0, c0 + DOWN_COLS)
        res_ref[:, cols] = res_ref[:, cols] + _dot(act_ref[...], wd_ref[:, cols])
        if side_work is not None:
            for _ in range(side_plan[N_FF_CHUNKS + c0 // DOWN_COLS]):
                next(side_work, None)
    if side_work is not None:
        for _ in side_work:
            pass


def _ffn_scratch(tm):
    return [pltpu.VMEM((tm, D_MODEL), BF16), pltpu.VMEM((tm, D_FF), BF16),
            pltpu.VMEM((SUBLANES, D_FF), F32)]


def _ab_out_ffn_kernel(h_ref, ya_ref, x0_ref, gate0_ref, xn_ref, gaten_ref,
                       lcw_ref, lcb_ref, wa_ref, ba_ref, wx_ref, bx_ref, lam_ref,
                       woa_ref, wob_ref, g_ref, wg_ref, wu_ref, cw_ref, cb_ref, wd_ref, o_ref,
                       ylru_ref, xprev_ref, hprev_ref, sa_ref, sb_ref, sh_ref,
                       hn_ref, act_ref, carry_ref, *, tiles_per_seq):
    i = pl.program_id(0)
    seq_tile = i % tiles_per_seq
    slot = i % 2
    lru = (lcw_ref, lcb_ref, wa_ref, ba_ref, wx_ref, bx_ref, lam_ref, xprev_ref, hprev_ref,
           sa_ref, sb_ref, sh_ref)

    @pl.when(i == 0)
    def _():
        for _ in _rglru_tile(x0_ref, gate0_ref, True, lru, ylru_ref, 0):
            pass

    next_starts_seq = (i + 1) % tiles_per_seq == 0
    side = _rglru_tile(xn_ref, gaten_ref, next_starts_seq, lru, ylru_ref, 1 - slot)
    next(side)
    o_ref[...] = h_ref[...] + _dot(ya_ref[...], woa_ref[...]) + _dot(ylru_ref[slot], wob_ref[...])
    _ffn(o_ref, seq_tile, g_ref, wg_ref, wu_ref, cw_ref, cb_ref, wd_ref,
         hn_ref, act_ref, carry_ref, side_work=side, side_plan=LRU_SIDE_PLAN)


def _ab_out_ffn(h, ya, x_lru, gate_lru, lru_consts, woa, wob, ffn, layer, seq_len):
    n = h.shape[0]
    tm = ROW_TILE
    n_tiles = n // tm
    row = lambda c: pl.BlockSpec((tm, c), lambda i: (i, 0))
    first = pl.BlockSpec((tm, LRU_WIDTH), lambda i: (0, 0), pipeline_mode=pl.Buffered(1))
    ahead = pl.BlockSpec((tm, LRU_WIDTH), lambda i: (jnp.minimum(i + 1, n_tiles - 1), 0))
    consts = (*lru_consts, woa, wob, *ffn)
    return pl.pallas_call(
        functools.partial(_ab_out_ffn_kernel, tiles_per_seq=seq_len // tm),
        grid=(n_tiles,),
        in_specs=[row(D_MODEL), row(ya.shape[1]), first, first, ahead, ahead]
                 + [_const_spec(a.shape) for a in (*lru_consts, woa, wob)]
                 + [_layer_spec(a.shape, layer) for a in ffn],
        out_specs=row(D_MODEL),
        out_shape=jax.ShapeDtypeStruct((n, D_MODEL), F32),
        scratch_shapes=[pltpu.VMEM((2, tm, LRU_WIDTH), BF16), pltpu.VMEM((SUBLANES, LRU_WIDTH), F32),
                        pltpu.VMEM((SUBLANES, LRU_WIDTH), F32)]
                       + [pltpu.VMEM((LRU_WIDTH // LANES, tm, LANES), F32)] * 3 + _ffn_scratch(tm),
        compiler_params=pltpu.CompilerParams(dimension_semantics=("arbitrary",),
                                             vmem_limit_bytes=VMEM_LIMIT),
        name="ab_out_ffn",
    )(h, ya, x_lru, gate_lru, x_lru, gate_lru, *consts)


def _sgu_ffn_kernel(h_ref, cg_ref, win_ref, lng_ref, lnb_ref, ws_ref, bs_ref, wout_ref,
                    g_ref, wg_ref, wu_ref, cw_ref, cb_ref, wd_ref, fg_ref, o_ref,
                    u_ref, v_ref, gated_ref, hn_ref, act_ref, carry_ref, *, tiles_per_seq):
    tm = h_ref.shape[0]
    seq_tile = pl.program_id(0) % tiles_per_seq
    h = h_ref[...]
    xn = _rms(h, cg_ref[...]).astype(BF16)
    v = _gelu(_dot(xn, win_ref[:, D_MODEL:]))
    mu = jnp.mean(v, axis=-1, keepdims=True)
    vc = v - mu
    var = jnp.mean(vc * vc, axis=-1, keepdims=True)
    v_ref[...] = (vc * lax.rsqrt(var + NORM_EPS) * lng_ref[...] + lnb_ref[...]).astype(BF16)
    u_ref[...] = _gelu(_dot(xn, win_ref[:, :D_MODEL]))

    n_chunks = tm // CHUNK
    r = lax.broadcasted_iota(jnp.int32, (CHUNK, CHUNK), 0)
    c = lax.broadcasted_iota(jnp.int32, (CHUNK, CHUNK), 1)
    for gp in range(SGU_GROUPS):
        lanes = slice(gp * CHUNK, (gp + 1) * CHUNK)
        w = jnp.where(c <= r, ws_ref[gp], 0.0).astype(BF16)
        rhs = jnp.concatenate([v_ref[ck * CHUNK:(ck + 1) * CHUNK, lanes] for ck in range(n_chunks)],
                              axis=1)
        sg = _dot(w, rhs) + bs_ref[:, gp:gp + 1]
        for ck in range(n_chunks):
            rows = slice(ck * CHUNK, (ck + 1) * CHUNK)
            gated_ref[rows, lanes] = (u_ref[rows, lanes] * sg[:, ck * CHUNK:(ck + 1) * CHUNK]).astype(BF16)

    o_ref[...] = h_ref[...] + _dot(gated_ref[...], wout_ref[...])
    _ffn(o_ref, seq_tile, g_ref, wg_ref, wu_ref, cw_ref, cb_ref, wd_ref, hn_ref, act_ref, carry_ref)
    o_ref[...] = _rms(o_ref[...], fg_ref[...])


def _sgu_ffn(h, cg, win, lng, lnb, ws, bs_t, wout, ffn, layer, fg, seq_len):
    n = h.shape[0]
    tm = ROW_TILE
    row = lambda c: pl.BlockSpec((tm, c), lambda i: (i, 0))
    sgu = (cg, win, lng, lnb, ws, bs_t, wout)
    consts = (*sgu, *ffn, fg)
    return pl.pallas_call(
        functools.partial(_sgu_ffn_kernel, tiles_per_seq=seq_len // tm),
        grid=(n // tm,),
        in_specs=[row(D_MODEL)] + [_const_spec(a.shape) for a in sgu]
                 + [_layer_spec(a.shape, layer) for a in ffn] + [_const_spec(fg.shape)],
        out_specs=row(D_MODEL),
        out_shape=jax.ShapeDtypeStruct((n, D_MODEL), F32),
        scratch_shapes=[pltpu.VMEM((tm, D_MODEL), F32), pltpu.VMEM((tm, D_MODEL), BF16),
                        pltpu.VMEM((tm, D_MODEL), BF16)] + _ffn_scratch(tm),
        compiler_params=pltpu.CompilerParams(dimension_semantics=("arbitrary",),
                                             vmem_limit_bytes=VMEM_LIMIT),
        name="sgu_ffn",
    )(h, *consts)


def _ffn_params(norm, w_gate, w_up, conv_w, conv_b, w_down):
    depth = norm.shape[0]
    return (norm.reshape(depth, 1, D_MODEL), w_gate.astype(BF16), (0.5 * w_up).astype(BF16),
            conv_w, conv_b.reshape(depth, 1, D_FF), w_down.astype(BF16))


def _block_diag(w):
    heads, blk, _ = w.shape
    eye = jnp.eye(heads, dtype=w.dtype)
    return (w[:, :, None, :] * eye[:, None, :, None]).reshape(heads * blk, heads * blk)


def kernel(x, positions, ab_norm, ab_w_in, ab_q_norm, ab_w_q_b, ab_kv_norm, ab_w_kv_b, ab_conv_w, ab_conv_b, ab_w_rg_a, ab_b_rg_a, ab_w_rg_x, ab_b_rg_x, ab_lambda, ab_w_out, c_norm, c_w_in, c_ln_g, c_ln_b, c_w_s, c_b_s, c_w_out, ffn_norm, ffn_w_gate, ffn_w_up, ffn_conv_w, ffn_conv_b, ffn_w_down, final_norm):
    b, s, d = x.shape
    n = b * s
    h = x.reshape(n, d)
    pos = positions.reshape(n, 1)

    w_in = ab_w_in[0]
    o2 = Q_LORA + KV_LORA
    o3 = o2 + QK_ROPE
    zeros = lambda c: jnp.zeros((d, c), w_in.dtype)
    w_in_p = jnp.concatenate([w_in[:, :o2], zeros(QK_NOPE), w_in[:, o2:o3],
                              zeros(HEAD_PAD - QK_NOPE - QK_ROPE), w_in[:, o3:]], axis=1).astype(BF16)
    qk = QK_NOPE + QK_ROPE
    half = QK_ROPE // 2
    wq3 = ab_w_q_b[0].reshape(Q_LORA, MLA_HEADS, qk)
    pad_q = lambda w: jnp.pad(w, ((0, 0), (0, 0), (0, HEAD_PAD - w.shape[-1]))
                              ).reshape(Q_LORA, MLA_HEADS * HEAD_PAD)
    wq_rot = jnp.concatenate([jnp.zeros_like(wq3[..., :QK_NOPE]), -wq3[..., QK_NOPE + half:],
                              wq3[..., QK_NOPE:QK_NOPE + half]], axis=-1)
    wq = jnp.concatenate([pad_q(wq3), pad_q(wq_rot)], axis=1).astype(BF16)
    wkv = ab_w_kv_b[0].reshape(KV_LORA, MLA_HEADS, QK_NOPE + V_HEAD)
    wk = jnp.pad(wkv[:, :, :QK_NOPE], ((0, 0), (0, 0), (0, HEAD_PAD - QK_NOPE))
                 ).reshape(KV_LORA, MLA_HEADS * HEAD_PAD).astype(BF16)
    wv = wkv[:, :, QK_NOPE:].reshape(KV_LORA, MLA_HEADS * V_HEAD).astype(BF16)
    freq = jnp.exp(-math.log(ROPE_BASE) * jnp.arange(half, dtype=F32) / half)
    invf = jnp.tile(freq, LANES // half).reshape(1, LANES)

    q, k, v, x_lru, gate_lru = _ab_in(
        h, pos, ab_norm[0].reshape(1, d), w_in_p, ab_q_norm[0].reshape(1, Q_LORA), wq,
        ab_kv_norm[0].reshape(1, KV_LORA), wk, wv, invf)

    y_mla = _attention(q.reshape(b, s, -1), k.reshape(b, s, -1), v.reshape(b, s, -1))
    lru_consts = (ab_conv_w[0], ab_conv_b[0].reshape(1, -1),
                  _block_diag(ab_w_rg_a[0]).astype(BF16), ab_b_rg_a[0].reshape(1, -1),
                  _block_diag(ab_w_rg_x[0]).astype(BF16), ab_b_rg_x[0].reshape(1, -1),
                  ab_lambda[0].reshape(1, -1))

    w_out = ab_w_out[0].astype(BF16)
    mla_w = MLA_HEADS * V_HEAD
    ffn = _ffn_params(ffn_norm, ffn_w_gate, ffn_w_up, ffn_conv_w, ffn_conv_b, ffn_w_down)
    h = _ab_out_ffn(h, y_mla.reshape(n, mla_w), x_lru, gate_lru, lru_consts,
                    w_out[:mla_w], 0.5 * w_out[mla_w:], ffn, 0, seq_len=s)

    out = _sgu_ffn(h, c_norm[0].reshape(1, d), c_w_in[0].astype(BF16), c_ln_g[0].reshape(1, -1),
                   c_ln_b[0].reshape(1, -1), c_w_s[0], c_b_s[0].T, c_w_out[0].astype(BF16),
                   ffn, 1, final_norm.reshape(1, d), seq_len=s)
    return out.reshape(b, s, d)
```

```python
import functools
import math

import jax
import jax.numpy as jnp
from jax import lax
from jax.experimental import pallas as pl
from jax.experimental.pallas import tpu as pltpu

F32 = jnp.float32
BF16 = jnp.bfloat16

D_MODEL = 1024
MLA_HEADS = 8
Q_LORA = 256
KV_LORA = 128
QK_NOPE = 64
QK_ROPE = 32
V_HEAD = 64
ROPE_BASE = 10000.0
LRU_WIDTH = 512
LRU_HEADS = 8
LRU_CONV = 4
LRU_C = 8.0
CHUNK = 128
SGU_GROUPS = 8
D_FF = 2816
FFN_CONV = 3
NORM_EPS = 1e-6

LANES = 128
SUBLANES = 8
MXU_TILE = 256
HEAD_PAD = 128
AB_IN_PAD = Q_LORA + KV_LORA + HEAD_PAD + 2 * LRU_WIDTH

ROW_TILE = 512
AB_IN_TILE = 1024
ATTN_TILE = 256
LRU_SIDE_PLAN = (2,) * 11 + (3, 3, 2, 2)
FF_CHUNK = 256
N_FF_CHUNKS = D_FF // FF_CHUNK
DOWN_COLS = 256
VMEM_LIMIT = 56 * 1024 * 1024


def _dot(a, b):
    return jnp.dot(a, b, preferred_element_type=F32)


def _dot_nt(a, b):
    return lax.dot_general(a, b, (((1,), (1,)), ((), ())), preferred_element_type=F32)


def _gelu_x2(x):
    c = math.sqrt(2.0 / math.pi)
    t = jnp.tanh(x * (c + (c * 0.044715) * (x * x)))
    return x * t + x


def _gelu(x):
    return 0.5 * _gelu_x2(x)


def _sigmoid(x):
    return 0.5 * jnp.tanh(0.5 * x) + 0.5


def _expm1(x):
    u = jnp.exp(x)
    small = jnp.where(u == 1.0, x, (u - 1.0) * x / jnp.log(u))
    return jnp.where(x < -0.5, u - 1.0, small)


def _rms(x, g):
    ms = jnp.mean(x * x, axis=-1, keepdims=True)
    return x * lax.rsqrt(ms + NORM_EPS) * g


def _const_spec(shape):
    nd = len(shape)
    return pl.BlockSpec(shape, lambda *_: (0,) * nd, pipeline_mode=pl.Buffered(1))


def _layer_spec(shape, layer):
    nd = len(shape)
    return pl.BlockSpec((None,) + tuple(shape[1:]), lambda *_: (layer,) + (0,) * (nd - 1),
                        pipeline_mode=pl.Buffered(1))


def _ab_in_kernel(h_ref, pos_ref, g_ref, w_in_ref, qg_ref, wq_ref, kvg_ref, wk_ref, wv_ref,
                  invf_ref, q_ref, k_ref, v_ref, xl_ref, gate_ref, *, scale, sub):
    tm = h_ref.shape[0]
    o1 = Q_LORA
    o2 = o1 + KV_LORA
    o3 = o2 + HEAD_PAD
    o4 = o3 + LRU_WIDTH
    half = QK_ROPE // 2
    x1_lo, x2_lo, x2_hi = QK_NOPE, QK_NOPE + half, QK_NOPE + QK_ROPE

    def rope(blk, c, sd, su):
        return blk * c + pltpu.roll(blk, half, 1) * sd + pltpu.roll(blk, LANES - half, 1) * su

    def norm_stage(r):
        groups = LANES // QK_ROPE
        nb = sub // groups
        lane = lax.broadcasted_iota(jnp.int32, (nb, LANES), 1)
        pos_c = pos_ref[r.start + (groups - 1) * nb:r.start + groups * nb, :]
        for gi in range(groups - 2, -1, -1):
            pos_c = jnp.where(lane < (gi + 1) * QK_ROPE,
                              pos_ref[r.start + gi * nb:r.start + (gi + 1) * nb, :], pos_c)
        ang = pos_c.astype(F32) * invf_ref[...]
        cos_c = jnp.cos(ang)
        sin_c = jnp.sin(ang)
        c_tab, s_dn, s_up = [], [], []
        for gi in range(groups):
            shift = (x1_lo - gi * QK_ROPE) % LANES
            cosv = cos_c if shift == 0 else pltpu.roll(cos_c, shift, 1)
            sinv = sin_c if shift == 0 else pltpu.roll(sin_c, shift, 1)
            c_tab.append(jnp.where(lane < x1_lo, 1.0, jnp.where(lane < x2_hi, cosv, 0.0)))
            s_dn.append(jnp.where((lane >= x2_lo) & (lane < x2_hi), sinv, 0.0))
            s_up.append(jnp.where((lane >= x1_lo) & (lane < x2_lo), -sinv, 0.0))
        tabs = tuple(jnp.concatenate(t, axis=0) for t in (c_tab, s_dn, s_up))
        xn = _rms(h_ref[r, :], g_ref[...]).astype(BF16)
        return r, xn, tabs

    def proj_stage(r, xn, tabs):
        xl_ref[r, :] = _dot(xn, w_in_ref[:, o3:o4])
        gate_ref[r, :] = _dot(xn, w_in_ref[:, o4:])
        c_q = _dot(xn, w_in_ref[:, :o1])
        c_kv = _dot(xn, w_in_ref[:, o1:o2])
        kpe = _dot(xn, w_in_ref[:, o2:o3])
        return r, c_q, c_kv, kpe, tabs

    def latent_stage(r, c_q, c_kv, kpe, tabs):
        qn = _rms(c_q, qg_ref[...]).astype(BF16)
        width = MLA_HEADS * HEAD_PAD
        qf = _dot(qn, wq_ref[:, :width])
        qr = _dot(qn, wq_ref[:, width:])
        kvn = _rms(c_kv, kvg_ref[...]).astype(BF16)
        kf = _dot(kvn, wk_ref[...])
        v_ref[r, :] = _dot(kvn, wv_ref[...]).astype(BF16)
        return r, qf, qr, kf, kpe, tabs

    def rope_stage(r, qf, qr, kf, kpe, tabs):
        c_tab, s_dn, s_up = tabs
        cq, sq = c_tab * scale, (s_dn - s_up) * scale
        kpe_r = rope(kpe, c_tab, s_dn, s_up)
        for hd in range(MLA_HEADS):
            sl = slice(hd * HEAD_PAD, (hd + 1) * HEAD_PAD)
            q_ref[r, sl] = (qf[:, sl] * cq + qr[:, sl] * sq).astype(BF16)
            k_ref[r, sl] = (kf[:, sl] + kpe_r).astype(BF16)

    stages = (proj_stage, latent_stage, rope_stage)
    n_sub = tm // sub
    live = [None] * len(stages)
    for step in range(n_sub + len(stages)):
        nxt = [None] * len(stages)
        if step < n_sub:
            nxt[0] = norm_stage(slice(step * sub, (step + 1) * sub))
        for si, stage in enumerate(stages):
            if live[si] is not None:
                out = stage(*live[si])
                if si + 1 < len(stages):
                    nxt[si + 1] = out
        live = nxt


def _ab_in(h, pos, g, w_in, qg, wq, kvg, wk, wv, invf):
    n = h.shape[0]
    tm = AB_IN_TILE
    row = lambda c: pl.BlockSpec((tm, c), lambda i: (i, 0))
    scale = float((QK_NOPE + QK_ROPE) ** -0.5 * math.log2(math.e))
    return pl.pallas_call(
        functools.partial(_ab_in_kernel, scale=scale, sub=ROW_TILE),
        grid=(n // tm,),
        in_specs=[row(D_MODEL), row(1), _const_spec(g.shape), _const_spec(w_in.shape),
                  _const_spec(qg.shape), _const_spec(wq.shape), _const_spec(kvg.shape),
                  _const_spec(wk.shape), _const_spec(wv.shape), _const_spec(invf.shape)],
        out_specs=[row(MLA_HEADS * HEAD_PAD), row(MLA_HEADS * HEAD_PAD), row(MLA_HEADS * V_HEAD),
                   row(LRU_WIDTH), row(LRU_WIDTH)],
        out_shape=[jax.ShapeDtypeStruct((n, MLA_HEADS * HEAD_PAD), BF16),
                   jax.ShapeDtypeStruct((n, MLA_HEADS * HEAD_PAD), BF16),
                   jax.ShapeDtypeStruct((n, MLA_HEADS * V_HEAD), BF16),
                   jax.ShapeDtypeStruct((n, LRU_WIDTH), F32),
                   jax.ShapeDtypeStruct((n, LRU_WIDTH), F32)],
        compiler_params=pltpu.CompilerParams(dimension_semantics=("arbitrary",),
                                             vmem_limit_bytes=VMEM_LIMIT),
        name="ab_in",
    )(h, pos, g, w_in, qg, wq, kvg, wk, wv, invf)


def _attn_kernel(q_ref, k_ref, v_ref, o_ref):
    s_len = q_ref.shape[0]
    t = ATTN_TILE
    r = lax.broadcasted_iota(jnp.int32, (t, t), 0)
    c = lax.broadcasted_iota(jnp.int32, (t, t), 1)
    causal = c <= r
    lane = lax.broadcasted_iota(jnp.int32, (t, LANES), 1)
    n_tiles = s_len // t

    def scores(i, hd):
        hl = slice(hd * HEAD_PAD, (hd + 1) * HEAD_PAD)
        return _dot_nt(q_ref[i * t:(i + 1) * t, hl], k_ref[:(i + 1) * t, hl])

    def softmax(i, s):
        kv = (i + 1) * t
        diag = jnp.where(causal, s[:, kv - t:], -jnp.inf)
        s = diag if i == 0 else jnp.concatenate([s[:, :kv - t], diag], axis=1)
        p = jnp.exp2(s - jnp.max(s, axis=1, keepdims=True))
        return p.astype(BF16), jnp.sum(p, axis=1, keepdims=True)

    def values(i, p, l):
        return _dot(p, v_ref[:(i + 1) * t, :]) / l

    s_cur = p_cur = None
    for step in range(n_tiles + 2):
        s_next = [scores(step, hd) for hd in range(2)] if step < n_tiles else None
        p_next = [softmax(step - 1, s) for s in s_cur] if s_cur is not None else None
        if p_cur is not None:
            i = step - 2
            o0, o1 = (values(i, p, l) for p, l in p_cur)
            o_ref[i * t:(i + 1) * t, :] = jnp.where(lane < V_HEAD, o0, o1).astype(o_ref.dtype)
        s_cur, p_cur = s_next, p_next


def _attention(q, k, v):
    b, s, _ = q.shape
    pairs = MLA_HEADS // 2
    return pl.pallas_call(
        _attn_kernel,
        grid=(b, pairs),
        in_specs=[pl.BlockSpec((None, s, 2 * HEAD_PAD), lambda bi, p: (bi, 0, p)),
                  pl.BlockSpec((None, s, 2 * HEAD_PAD), lambda bi, p: (bi, 0, p)),
                  pl.BlockSpec((None, s, 2 * V_HEAD), lambda bi, p: (bi, 0, p))],
        out_specs=pl.BlockSpec((None, s, 2 * V_HEAD), lambda bi, p: (bi, 0, p)),
        out_shape=jax.ShapeDtypeStruct((b, s, MLA_HEADS * V_HEAD), BF16),
        compiler_params=pltpu.CompilerParams(dimension_semantics=("arbitrary", "arbitrary"),
                                             vmem_limit_bytes=VMEM_LIMIT),
        name="mla_attn",
    )(q, k, v)


def _rglru_tile(x_ref, gate_ref, first, lru, y_ref, slot):
    (cw_ref, cb_ref, wa_ref, ba_ref, wx_ref, bx_ref, lam_ref,
     xprev_ref, hprev_ref, a_ref, b_ref, h_ref) = lru
    ts, c = x_ref.shape
    slabs = c // LANES
    if first is True:
        h_in = [jnp.zeros((SUBLANES, LANES), F32)] * slabs
    else:
        h_in = [jnp.where(first, 0.0, hprev_ref[:, j * LANES:(j + 1) * LANES]) for j in range(slabs)]
    xcs = []
    for j in range(slabs):
        lanes = slice(j * LANES, (j + 1) * LANES)
        xj = x_ref[:, lanes]
        if first is True:
            xprev_ref[j, :SUBLANES, :] = jnp.zeros((SUBLANES, LANES), F32)
        else:
            xprev_ref[j, :SUBLANES, :] = jnp.where(first, 0.0, xprev_ref[j, :SUBLANES, :])
        xprev_ref[j, SUBLANES:, :] = xj
        xcj = cb_ref[:, lanes] + cw_ref[LRU_CONV - 1:LRU_CONV, lanes] * xj
        for k in range(1, LRU_CONV):
            tap = xprev_ref[j, pl.ds(SUBLANES - k, ts, stride=1), :]
            xcj = xcj + cw_ref[LRU_CONV - 1 - k:LRU_CONV - k, lanes] * tap
        xprev_ref[j, :SUBLANES, :] = xj[ts - SUBLANES:]
        xcs.append(xcj)
    xc = jnp.concatenate(xcs, axis=1)
    xb = xc.astype(BF16)

    def gate_pre(w_ref, b_ref):
        parts = [_dot(xb[:, t0:t0 + MXU_TILE], w_ref[t0:t0 + MXU_TILE, t0:t0 + MXU_TILE])
                 for t0 in range(0, c, MXU_TILE)]
        return jnp.concatenate(parts, axis=1) + b_ref[...]

    pre_a = gate_pre(wa_ref, ba_ref)
    pre_x = gate_pre(wx_ref, bx_ref)
    z = -lam_ref[...]
    softplus = jnp.maximum(z, 0.0) + jnp.log1p(jnp.exp(-jnp.abs(z)))
    row = lax.broadcasted_iota(jnp.int32, (SUBLANES, LANES), 0)
    yield

    blk = SUBLANES * SUBLANES
    for r0 in range(0, ts, blk):
        rows = slice(r0, r0 + blk)
        for j in range(slabs):
            lanes = slice(j * LANES, (j + 1) * LANES)
            log_a = (-LRU_C) * _sigmoid(pre_a[rows, lanes]) * softplus[:, lanes]
            a_ref[j, rows, :] = jnp.exp(log_a)
            b_ref[j, rows, :] = (jnp.sqrt(-_expm1(2.0 * log_a))
                                 * (_sigmoid(pre_x[rows, lanes]) * xc[rows, lanes]))
            hs, ps = [], []
            for r in range(SUBLANES):
                a = a_ref[j, pl.ds(r0 + r, SUBLANES, stride=SUBLANES), :]
                b = b_ref[j, pl.ds(r0 + r, SUBLANES, stride=SUBLANES), :]
                hs.append(b if r == 0 else a * hs[-1] + b)
                ps.append(a if r == 0 else a * ps[-1])
            p, q = ps[-1], hs[-1]
            for k in (1, 2, 4):
                p_sh = jnp.where(row >= k, pltpu.roll(p, k, 0), 1.0)
                q_sh = jnp.where(row >= k, pltpu.roll(q, k, 0), 0.0)
                q = p * q_sh + q
                p = p * p_sh
            seg_out = p * h_in[j] + q
            seg_in = jnp.where(row == 0, h_in[j], pltpu.roll(seg_out, 1, 0))
            for r in range(SUBLANES):
                h_ref[j, pl.ds(r0 + r, SUBLANES, stride=SUBLANES), :] = ps[r] * seg_in + hs[r]
            h_in[j] = jnp.broadcast_to(seg_out[SUBLANES - 1:, :], (SUBLANES, LANES))
            y = h_ref[j, rows, :] * _gelu_x2(gate_ref[rows, lanes])
            y_ref[slot, rows, lanes] = y.astype(y_ref.dtype)
            yield
    hprev_ref[...] = jnp.concatenate(h_in, axis=1)


def _ffn(res_ref, seq_tile, g_ref, wg_ref, wu_ref, cw_ref, cb_ref, wd_ref, hn_ref, act_ref, carry_ref,
         side_work=None, side_plan=None):
    tm = res_ref.shape[0]
    hn_ref[...] = _rms(res_ref[...], g_ref[...]).astype(BF16)

    @pl.when(seq_tile == 0)
    def _():
        carry_ref[:, :SUBLANES, :] = jnp.zeros((carry_ref.shape[0], SUBLANES, LANES), F32)

    for f in range(N_FF_CHUNKS):
        cols = slice(f * FF_CHUNK, (f + 1) * FF_CHUNK)
        hn = hn_ref[...]
        g = _dot(hn, wg_ref[:, cols])
        u = _dot(hn, wu_ref[:, cols])
        ys = []
        for j in range(FF_CHUNK // LANES):
            slab = f * (FF_CHUNK // LANES) + j
            lanes = slice(cols.start + j * LANES, cols.start + (j + 1) * LANES)
            gj = g[:, j * LANES:(j + 1) * LANES]
            carry_ref[slab, SUBLANES:, :] = gj
            yj = cb_ref[:, lanes] + cw_ref[FFN_CONV - 1:FFN_CONV, lanes] * gj
            for k in range(1, FFN_CONV):
                tap = carry_ref[slab, pl.ds(SUBLANES - k, tm, stride=1), :]
                yj = yj + cw_ref[FFN_CONV - 1 - k:FFN_CONV - k, lanes] * tap
            carry_ref[slab, :SUBLANES, :] = gj[tm - SUBLANES:]
            ys.append(yj)
        y = jnp.concatenate(ys, axis=1)
        act_ref[:, cols] = (_gelu_x2(y) * u).astype(BF16)
        if side_work is not None:
            for _ in range(side_plan[f]):
                next(side_work, None)
    for c0 in range(0, D_MODEL, DOWN_COLS):
        cols = slice(c0, c0 + DOWN_COLS)
        res_ref[:, cols] = res_ref[:, cols] + _dot(act_ref[...], wd_ref[:, cols])
        if side_work is not None:
            for _ in range(side_plan[N_FF_CHUNKS + c0 // DOWN_COLS]):
                next(side_work, None)
    if side_work is not None:
        for _ in side_work:
            pass


def _ffn_scratch(tm):
    return [pltpu.VMEM((tm, D_MODEL), BF16), pltpu.VMEM((tm, D_FF), BF16),
            pltpu.VMEM((D_FF // LANES, SUBLANES + tm, LANES), F32)]


def _ab_out_ffn_kernel(h_ref, ya_ref, x0_ref, gate0_ref, xn_ref, gaten_ref,
                       lcw_ref, lcb_ref, wa_ref, ba_ref, wx_ref, bx_ref, lam_ref,
                       woa_ref, wob_ref, g_ref, wg_ref, wu_ref, cw_ref, cb_ref, wd_ref, o_ref,
                       ylru_ref, xprev_ref, hprev_ref, sa_ref, sb_ref, sh_ref,
                       hn_ref, act_ref, carry_ref, *, tiles_per_seq):
    i = pl.program_id(0)
    seq_tile = i % tiles_per_seq
    slot = i % 2
    lru = (lcw_ref, lcb_ref, wa_ref, ba_ref, wx_ref, bx_ref, lam_ref, xprev_ref, hprev_ref,
           sa_ref, sb_ref, sh_ref)

    @pl.when(i == 0)
    def _():
        for _ in _rglru_tile(x0_ref, gate0_ref, True, lru, ylru_ref, 0):
            pass

    next_starts_seq = (i + 1) % tiles_per_seq == 0
    side = _rglru_tile(xn_ref, gaten_ref, next_starts_seq, lru, ylru_ref, 1 - slot)
    next(side)
    o_ref[...] = h_ref[...] + _dot(ya_ref[...], woa_ref[...]) + _dot(ylru_ref[slot], wob_ref[...])
    _ffn(o_ref, seq_tile, g_ref, wg_ref, wu_ref, cw_ref, cb_ref, wd_ref,
         hn_ref, act_ref, carry_ref, side_work=side, side_plan=LRU_SIDE_PLAN)


def _ab_out_ffn(h, ya, x_lru, gate_lru, lru_consts, woa, wob, ffn, layer, seq_len):
    n = h.shape[0]
    tm = ROW_TILE
    n_tiles = n // tm
    row = lambda c: pl.BlockSpec((tm, c), lambda i: (i, 0))
    first = pl.BlockSpec((tm, LRU_WIDTH), lambda i: (0, 0), pipeline_mode=pl.Buffered(1))
    ahead = pl.BlockSpec((tm, LRU_WIDTH), lambda i: (jnp.minimum(i + 1, n_tiles - 1), 0))
    consts = (*lru_consts, woa, wob, *ffn)
    return pl.pallas_call(
        functools.partial(_ab_out_ffn_kernel, tiles_per_seq=seq_len // tm),
        grid=(n_tiles,),
        in_specs=[row(D_MODEL), row(ya.shape[1]), first, first, ahead, ahead]
                 + [_const_spec(a.shape) for a in (*lru_consts, woa, wob)]
                 + [_layer_spec(a.shape, layer) for a in ffn],
        out_specs=row(D_MODEL),
        out_shape=jax.ShapeDtypeStruct((n, D_MODEL), F32),
        scratch_shapes=[pltpu.VMEM((2, tm, LRU_WIDTH), BF16),
                        pltpu.VMEM((LRU_WIDTH // LANES, SUBLANES + tm, LANES), F32),
                        pltpu.VMEM((SUBLANES, LRU_WIDTH), F32)]
                       + [pltpu.VMEM((LRU_WIDTH // LANES, tm, LANES), F32)] * 3 + _ffn_scratch(tm),
        compiler_params=pltpu.CompilerParams(dimension_semantics=("arbitrary",),
                                             vmem_limit_bytes=VMEM_LIMIT),
        name="ab_out_ffn",
    )(h, ya, x_lru, gate_lru, x_lru, gate_lru, *consts)


def _sgu_ffn_kernel(h_ref, cg_ref, win_ref, lng_ref, lnb_ref, ws_ref, bs_ref, wout_ref,
                    g_ref, wg_ref, wu_ref, cw_ref, cb_ref, wd_ref, fg_ref, o_ref,
                    u_ref, v_ref, gated_ref, hn_ref, act_ref, carry_ref, *, tiles_per_seq):
    tm = h_ref.shape[0]
    seq_tile = pl.program_id(0) % tiles_per_seq
    h = h_ref[...]
    xn = _rms(h, cg_ref[...]).astype(BF16)
    v = _gelu(_dot(xn, win_ref[:, D_MODEL:]))
    mu = jnp.mean(v, axis=-1, keepdims=True)
    vc = v - mu
    var = jnp.mean(vc * vc, axis=-1, keepdims=True)
    v_ref[...] = (vc * lax.rsqrt(var + NORM_EPS) * lng_ref[...] + lnb_ref[...]).astype(BF16)
    u_ref[...] = _gelu(_dot(xn, win_ref[:, :D_MODEL]))

    n_chunks = tm // CHUNK
    r = lax.broadcasted_iota(jnp.int32, (CHUNK, CHUNK), 0)
    c = lax.broadcasted_iota(jnp.int32, (CHUNK, CHUNK), 1)
    for gp in range(SGU_GROUPS):
        lanes = slice(gp * CHUNK, (gp + 1) * CHUNK)
        w = jnp.where(c <= r, ws_ref[gp], 0.0).astype(BF16)
        rhs = jnp.concatenate([v_ref[ck * CHUNK:(ck + 1) * CHUNK, lanes] for ck in range(n_chunks)],
                              axis=1)
        sg = _dot(w, rhs) + bs_ref[:, gp:gp + 1]
        for ck in range(n_chunks):
            rows = slice(ck * CHUNK, (ck + 1) * CHUNK)
            gated_ref[rows, lanes] = (u_ref[rows, lanes] * sg[:, ck * CHUNK:(ck + 1) * CHUNK]).astype(BF16)

    o_ref[...] = h_ref[...] + _dot(gated_ref[...], wout_ref[...])
    _ffn(o_ref, seq_tile, g_ref, wg_ref, wu_ref, cw_ref, cb_ref, wd_ref, hn_ref, act_ref, carry_ref)
    o_ref[...] = _rms(o_ref[...], fg_ref[...])


def _sgu_ffn(h, cg, win, lng, lnb, ws, bs_t, wout, ffn, layer, fg, seq_len):
    n = h.shape[0]
    tm = ROW_TILE
    row = lambda c: pl.BlockSpec((tm, c), lambda i: (i, 0))
    sgu = (cg, win, lng, lnb, ws, bs_t, wout)
    consts = (*sgu, *ffn, fg)
    return pl.pallas_call(
        functools.partial(_sgu_ffn_kernel, tiles_per_seq=seq_len // tm),
        grid=(n // tm,),
        in_specs=[row(D_MODEL)] + [_const_spec(a.shape) for a in sgu]
                 + [_layer_spec(a.shape, layer) for a in ffn] + [_const_spec(fg.shape)],
        out_specs=row(D_MODEL),
        out_shape=jax.ShapeDtypeStruct((n, D_MODEL), F32),
        scratch_shapes=[pltpu.VMEM((tm, D_MODEL), F32), pltpu.VMEM((tm, D_MODEL), BF16),
                        pltpu.VMEM((tm, D_MODEL), BF16)] + _ffn_scratch(tm),
        compiler_params=pltpu.CompilerParams(dimension_semantics=("arbitrary",),
                                             vmem_limit_bytes=VMEM_LIMIT),
        name="sgu_ffn",
    )(h, *consts)


def _ffn_params(norm, w_gate, w_up, conv_w, conv_b, w_down):
    depth = norm.shape[0]
    return (norm.reshape(depth, 1, D_MODEL), w_gate.astype(BF16), (0.5 * w_up).astype(BF16),
            conv_w, conv_b.reshape(depth, 1, D_FF), w_down.astype(BF16))


def _block_diag(w):
    heads, blk, _ = w.shape
    eye = jnp.eye(heads, dtype=w.dtype)
    return (w[:, :, None, :] * eye[:, None, :, None]).reshape(heads * blk, heads * blk)


def kernel(x, positions, ab_norm, ab_w_in, ab_q_norm, ab_w_q_b, ab_kv_norm, ab_w_kv_b, ab_conv_w, ab_conv_b, ab_w_rg_a, ab_b_rg_a, ab_w_rg_x, ab_b_rg_x, ab_lambda, ab_w_out, c_norm, c_w_in, c_ln_g, c_ln_b, c_w_s, c_b_s, c_w_out, ffn_norm, ffn_w_gate, ffn_w_up, ffn_conv_w, ffn_conv_b, ffn_w_down, final_norm):
    b, s, d = x.shape
    n = b * s
    h = x.reshape(n, d)
    pos = positions.reshape(n, 1)

    w_in = ab_w_in[0]
    o2 = Q_LORA + KV_LORA
    o3 = o2 + QK_ROPE
    zeros = lambda c: jnp.zeros((d, c), w_in.dtype)
    w_in_p = jnp.concatenate([w_in[:, :o2], zeros(QK_NOPE), w_in[:, o2:o3],
                              zeros(HEAD_PAD - QK_NOPE - QK_ROPE), w_in[:, o3:]], axis=1).astype(BF16)
    qk = QK_NOPE + QK_ROPE
    half = QK_ROPE // 2
    wq3 = ab_w_q_b[0].reshape(Q_LORA, MLA_HEADS, qk)
    pad_q = lambda w: jnp.pad(w, ((0, 0), (0, 0), (0, HEAD_PAD - w.shape[-1]))
                              ).reshape(Q_LORA, MLA_HEADS * HEAD_PAD)
    wq_rot = jnp.concatenate([jnp.zeros_like(wq3[..., :QK_NOPE]), -wq3[..., QK_NOPE + half:],
                              wq3[..., QK_NOPE:QK_NOPE + half]], axis=-1)
    wq = jnp.concatenate([pad_q(wq3), pad_q(wq_rot)], axis=1).astype(BF16)
    wkv = ab_w_kv_b[0].reshape(KV_LORA, MLA_HEADS, QK_NOPE + V_HEAD)
    wk = jnp.pad(wkv[:, :, :QK_NOPE], ((0, 0), (0, 0), (0, HEAD_PAD - QK_NOPE))
                 ).reshape(KV_LORA, MLA_HEADS * HEAD_PAD).astype(BF16)
    wv = wkv[:, :, QK_NOPE:].reshape(KV_LORA, MLA_HEADS * V_HEAD).astype(BF16)
    freq = jnp.exp(-math.log(ROPE_BASE) * jnp.arange(half, dtype=F32) / half)
    invf = jnp.tile(freq, LANES // half).reshape(1, LANES)

    q, k, v, x_lru, gate_lru = _ab_in(
        h, pos, ab_norm[0].reshape(1, d), w_in_p, ab_q_norm[0].reshape(1, Q_LORA), wq,
        ab_kv_norm[0].reshape(1, KV_LORA), wk, wv, invf)

    y_mla = _attention(q.reshape(b, s, -1), k.reshape(b, s, -1), v.reshape(b, s, -1))
    lru_consts = (ab_conv_w[0], ab_conv_b[0].reshape(1, -1),
                  _block_diag(ab_w_rg_a[0]).astype(BF16), ab_b_rg_a[0].reshape(1, -1),
                  _block_diag(ab_w_rg_x[0]).astype(BF16), ab_b_rg_x[0].reshape(1, -1),
                  ab_lambda[0].reshape(1, -1))

    w_out = ab_w_out[0].astype(BF16)
    mla_w = MLA_HEADS * V_HEAD
    ffn = _ffn_params(ffn_norm, ffn_w_gate, ffn_w_up, ffn_conv_w, ffn_conv_b, ffn_w_down)
    h = _ab_out_ffn(h, y_mla.reshape(n, mla_w), x_lru, gate_lru, lru_consts,
                    w_out[:mla_w], 0.5 * w_out[mla_w:], ffn, 0, seq_len=s)

    out = _sgu_ffn(h, c_norm[0].reshape(1, d), c_w_in[0].astype(BF16), c_ln_g[0].reshape(1, -1),
                   c_ln_b[0].reshape(1, -1), c_w_s[0], c_b_s[0].T, c_w_out[0].astype(BF16),
                   ffn, 1, final_norm.reshape(1, d), seq_len=s)
    return out.reshape(b, s, d)
```

```python
import functools
import math

import jax
import jax.numpy as jnp
from jax import lax
from jax.experimental import pallas as pl
from jax.experimental.pallas import tpu as pltpu

F32 = jnp.float32
BF16 = jnp.bfloat16

D_MODEL = 1024
MLA_HEADS = 8
Q_LORA = 256
KV_LORA = 128
QK_NOPE = 64
QK_ROPE = 32
V_HEAD = 64
ROPE_BASE = 10000.0
LRU_WIDTH = 512
LRU_CONV = 4
LRU_C = 8.0
CHUNK = 128
SGU_GROUPS = 8
D_FF = 2816
FFN_CONV = 3
NORM_EPS = 1e-6

LANES = 128
SUBLANES = 8
MXU_TILE = 256
HEAD_PAD = 128

ROW_TILE = 512
AB_IN_TILE = 1024
SGU_STEP_ROWS = 512
ATTN_TILE = 256
ATTN_PAIRS_PER_STEP = 1
LRU_SIDE_PLAN = (2,) * 11 + (3, 3, 2, 2)
FF_CHUNK = 256
N_FF_CHUNKS = D_FF // FF_CHUNK
DOWN_COLS = 256
VMEM_LIMIT = 56 * 1024 * 1024


def _dot(a, b):
    return jnp.dot(a, b, preferred_element_type=F32)


def _dot_nt(a, b):
    return lax.dot_general(a, b, (((1,), (1,)), ((), ())), preferred_element_type=F32)


def _gelu_x2(x):
    c = math.sqrt(2.0 / math.pi)
    t = jnp.tanh(x * (c + (c * 0.044715) * (x * x)))
    return x * t + x


def _sigmoid(x):
    return 0.5 * jnp.tanh(0.5 * x) + 0.5


def _expm1(x):
    u = jnp.exp(x)
    small = jnp.where(u == 1.0, x, (u - 1.0) * x / jnp.log(u))
    return jnp.where(x < -0.5, u - 1.0, small)


def _rms(x, g):
    ms = jnp.mean(x * x, axis=-1, keepdims=True)
    return x * lax.rsqrt(ms + NORM_EPS) * g


def _const_spec(shape):
    nd = len(shape)
    return pl.BlockSpec(shape, lambda *_: (0,) * nd, pipeline_mode=pl.Buffered(1))


def _layer_spec(shape, layer):
    nd = len(shape)
    return pl.BlockSpec((None,) + tuple(shape[1:]), lambda *_: (layer,) + (0,) * (nd - 1),
                        pipeline_mode=pl.Buffered(1))


def _ab_in_kernel(h_ref, pos_ref, g_ref, w_in_ref, qg_ref, wq_ref, kvg_ref, wk_ref, wv_ref,
                  invf_ref, q_ref, k_ref, v_ref, xl_ref, gate_ref, *, scale, sub):
    tm = h_ref.shape[0]
    o1 = Q_LORA
    o2 = o1 + KV_LORA
    o3 = o2 + HEAD_PAD
    o4 = o3 + LRU_WIDTH
    half = QK_ROPE // 2
    x1_lo, x2_lo, x2_hi = QK_NOPE, QK_NOPE + half, QK_NOPE + QK_ROPE

    def rope(blk, c, sd, su):
        return blk * c + pltpu.roll(blk, half, 1) * sd + pltpu.roll(blk, LANES - half, 1) * su

    def norm_stage(r):
        groups = LANES // QK_ROPE
        nb = sub // groups
        lane = lax.broadcasted_iota(jnp.int32, (nb, LANES), 1)
        pos_c = pos_ref[r.start + (groups - 1) * nb:r.start + groups * nb, :]
        for gi in range(groups - 2, -1, -1):
            pos_c = jnp.where(lane < (gi + 1) * QK_ROPE,
                              pos_ref[r.start + gi * nb:r.start + (gi + 1) * nb, :], pos_c)
        ang = pos_c.astype(F32) * invf_ref[...]
        cos_c = jnp.cos(ang)
        sin_c = jnp.sin(ang)
        c_tab, s_dn, s_up = [], [], []
        for gi in range(groups):
            shift = (x1_lo - gi * QK_ROPE) % LANES
            cosv = cos_c if shift == 0 else pltpu.roll(cos_c, shift, 1)
            sinv = sin_c if shift == 0 else pltpu.roll(sin_c, shift, 1)
            c_tab.append(jnp.where(lane < x1_lo, 1.0, jnp.where(lane < x2_hi, cosv, 0.0)))
            s_dn.append(jnp.where((lane >= x2_lo) & (lane < x2_hi), sinv, 0.0))
            s_up.append(jnp.where((lane >= x1_lo) & (lane < x2_lo), -sinv, 0.0))
        tabs = tuple(jnp.concatenate(t, axis=0) for t in (c_tab, s_dn, s_up))
        xn = _rms(h_ref[r, :], g_ref[...]).astype(BF16)
        return r, xn, tabs

    def proj_stage(r, xn, tabs):
        xl_ref[r, :] = _dot(xn, w_in_ref[:, o3:o4])
        gate_ref[r, :] = _dot(xn, w_in_ref[:, o4:])
        c_q = _dot(xn, w_in_ref[:, :o1])
        c_kv = _dot(xn, w_in_ref[:, o1:o2])
        kpe = _dot(xn, w_in_ref[:, o2:o3])
        return r, c_q, c_kv, kpe, tabs

    def latent_stage(r, c_q, c_kv, kpe, tabs):
        qn = _rms(c_q, qg_ref[...]).astype(BF16)
        width = MLA_HEADS * HEAD_PAD
        qf = _dot(qn, wq_ref[:, :width])
        qr = _dot(qn, wq_ref[:, width:])
        kvn = _rms(c_kv, kvg_ref[...]).astype(BF16)
        kf = _dot(kvn, wk_ref[...])
        v_ref[r, :] = _dot(kvn, wv_ref[...]).astype(BF16)
        return r, qf, qr, kf, kpe, tabs

    def rope_stage(r, qf, qr, kf, kpe, tabs):
        c_tab, s_dn, s_up = tabs
        cq, sq = c_tab * scale, (s_dn - s_up) * scale
        kpe_r = rope(kpe, c_tab, s_dn, s_up)
        for hd in range(MLA_HEADS):
            sl = slice(hd * HEAD_PAD, (hd + 1) * HEAD_PAD)
            q_ref[r, sl] = (qf[:, sl] * cq + qr[:, sl] * sq).astype(BF16)
            k_ref[r, sl] = (kf[:, sl] + kpe_r).astype(BF16)

    stages = (proj_stage, latent_stage, rope_stage)
    n_sub = tm // sub
    live = [None] * len(stages)
    for step in range(n_sub + len(stages)):
        nxt = [None] * len(stages)
        if step < n_sub:
            nxt[0] = norm_stage(slice(step * sub, (step + 1) * sub))
        for si, stage in enumerate(stages):
            if live[si] is not None:
                out = stage(*live[si])
                if si + 1 < len(stages):
                    nxt[si + 1] = out
        live = nxt


def _ab_in(h, pos, g, w_in, qg, wq, kvg, wk, wv, invf):
    n = h.shape[0]
    tm = AB_IN_TILE
    row = lambda c: pl.BlockSpec((tm, c), lambda i: (i, 0))
    scale = float((QK_NOPE + QK_ROPE) ** -0.5 * math.log2(math.e))
    return pl.pallas_call(
        functools.partial(_ab_in_kernel, scale=scale, sub=ROW_TILE),
        grid=(n // tm,),
        in_specs=[row(D_MODEL), row(1), _const_spec(g.shape), _const_spec(w_in.shape),
                  _const_spec(qg.shape), _const_spec(wq.shape), _const_spec(kvg.shape),
                  _const_spec(wk.shape), _const_spec(wv.shape), _const_spec(invf.shape)],
        out_specs=[row(MLA_HEADS * HEAD_PAD), row(MLA_HEADS * HEAD_PAD), row(MLA_HEADS * V_HEAD),
                   row(LRU_WIDTH), row(LRU_WIDTH)],
        out_shape=[jax.ShapeDtypeStruct((n, MLA_HEADS * HEAD_PAD), BF16),
                   jax.ShapeDtypeStruct((n, MLA_HEADS * HEAD_PAD), BF16),
                   jax.ShapeDtypeStruct((n, MLA_HEADS * V_HEAD), BF16),
                   jax.ShapeDtypeStruct((n, LRU_WIDTH), F32),
                   jax.ShapeDtypeStruct((n, LRU_WIDTH), F32)],
        compiler_params=pltpu.CompilerParams(dimension_semantics=("arbitrary",),
                                             vmem_limit_bytes=VMEM_LIMIT),
        name="ab_in",
    )(h, pos, g, w_in, qg, wq, kvg, wk, wv, invf)


def _attn_kernel(q_ref, k_ref, v_ref, o_ref):
    s_len = q_ref.shape[0]
    t = ATTN_TILE
    r = lax.broadcasted_iota(jnp.int32, (t, t), 0)
    c = lax.broadcasted_iota(jnp.int32, (t, t), 1)
    causal = c <= r
    lane = lax.broadcasted_iota(jnp.int32, (t, LANES), 1)
    n_tiles = s_len // t

    def scores(i, hd):
        hl = slice(hd * HEAD_PAD, (hd + 1) * HEAD_PAD)
        return _dot_nt(q_ref[i * t:(i + 1) * t, hl], k_ref[:(i + 1) * t, hl])

    def softmax(i, s):
        kv = (i + 1) * t
        diag = jnp.where(causal, s[:, kv - t:], -jnp.inf)
        s = diag if i == 0 else jnp.concatenate([s[:, :kv - t], diag], axis=1)
        p = jnp.exp2(s - jnp.max(s, axis=1, keepdims=True))
        return p.astype(BF16), jnp.sum(p, axis=1, keepdims=True)

    def values(i, pair, p, l):
        return _dot(p, v_ref[:(i + 1) * t, pair * LANES:(pair + 1) * LANES]) / l

    n_pairs = q_ref.shape[1] // (2 * HEAD_PAD)
    order = [(pair, i) for pair in range(n_pairs) for i in range(n_tiles - 1, -1, -1)]
    s_cur = p_cur = None
    for step in range(len(order) + 2):
        s_next = p_next = None
        if step < len(order):
            pair, i = order[step]
            s_next = ([scores(i, 2 * pair + hd) for hd in range(2)], pair, i)
        if s_cur is not None:
            p_next = ([softmax(s_cur[2], s) for s in s_cur[0]], s_cur[1], s_cur[2])
        if p_cur is not None:
            _, pair, i = p_cur
            o0, o1 = (values(i, pair, p, l) for p, l in p_cur[0])
            o_ref[i * t:(i + 1) * t, pair * LANES:(pair + 1) * LANES] = jnp.where(
                lane < V_HEAD, o0, o1).astype(o_ref.dtype)
        s_cur, p_cur = s_next, p_next


def _attention(q, k, v):
    b, s, _ = q.shape
    pp = ATTN_PAIRS_PER_STEP
    steps = MLA_HEADS // (2 * pp)
    return pl.pallas_call(
        _attn_kernel,
        grid=(b, steps),
        in_specs=[pl.BlockSpec((None, s, pp * 2 * HEAD_PAD), lambda bi, p: (bi, 0, p)),
                  pl.BlockSpec((None, s, pp * 2 * HEAD_PAD), lambda bi, p: (bi, 0, p)),
                  pl.BlockSpec((None, s, pp * 2 * V_HEAD), lambda bi, p: (bi, 0, p))],
        out_specs=pl.BlockSpec((None, s, pp * 2 * V_HEAD), lambda bi, p: (bi, 0, p)),
        out_shape=jax.ShapeDtypeStruct((b, s, MLA_HEADS * V_HEAD), BF16),
        compiler_params=pltpu.CompilerParams(dimension_semantics=("arbitrary", "arbitrary"),
                                             vmem_limit_bytes=VMEM_LIMIT),
        name="mla_attn",
    )(q, k, v)


def _rglru_tile(x_ref, gate_ref, first, lru, y_ref, slot):
    (cw_ref, cb_ref, wa_ref, ba_ref, wx_ref, bx_ref, lam_ref,
     xprev_ref, hprev_ref, a_ref, b_ref, h_ref) = lru
    ts, c = x_ref.shape
    slabs = c // LANES
    if first is True:
        h_in = [jnp.zeros((SUBLANES, LANES), F32)] * slabs
    else:
        h_in = [jnp.where(first, 0.0, hprev_ref[:, j * LANES:(j + 1) * LANES]) for j in range(slabs)]
    xcs = []
    for j in range(slabs):
        lanes = slice(j * LANES, (j + 1) * LANES)
        xj = x_ref[:, lanes]
        if first is True:
            xprev_ref[j, :SUBLANES, :] = jnp.zeros((SUBLANES, LANES), F32)
        else:
            xprev_ref[j, :SUBLANES, :] = jnp.where(first, 0.0, xprev_ref[j, :SUBLANES, :])
        xprev_ref[j, SUBLANES:, :] = xj
        xcj = cb_ref[:, lanes] + cw_ref[LRU_CONV - 1:LRU_CONV, lanes] * xj
        for k in range(1, LRU_CONV):
            tap = xprev_ref[j, pl.ds(SUBLANES - k, ts, stride=1), :]
            xcj = xcj + cw_ref[LRU_CONV - 1 - k:LRU_CONV - k, lanes] * tap
        xprev_ref[j, :SUBLANES, :] = xj[ts - SUBLANES:]
        xcs.append(xcj)
    xc = jnp.concatenate(xcs, axis=1)
    xb = xc.astype(BF16)

    def gate_pre(w_ref, b_ref):
        parts = [_dot(xb[:, t0:t0 + MXU_TILE], w_ref[t0:t0 + MXU_TILE, t0:t0 + MXU_TILE])
                 for t0 in range(0, c, MXU_TILE)]
        return jnp.concatenate(parts, axis=1) + b_ref[...]

    pre_a = gate_pre(wa_ref, ba_ref)
    pre_x = gate_pre(wx_ref, bx_ref)
    z = -lam_ref[...]
    softplus = jnp.maximum(z, 0.0) + jnp.log1p(jnp.exp(-jnp.abs(z)))
    row = lax.broadcasted_iota(jnp.int32, (SUBLANES, LANES), 0)
    yield

    blk = SUBLANES * SUBLANES
    for r0 in range(0, ts, blk):
        rows = slice(r0, r0 + blk)
        for j in range(slabs):
            lanes = slice(j * LANES, (j + 1) * LANES)
            log_a = (-LRU_C) * _sigmoid(pre_a[rows, lanes]) * softplus[:, lanes]
            a_ref[j, rows, :] = jnp.exp(log_a)
            b_ref[j, rows, :] = (jnp.sqrt(-_expm1(2.0 * log_a))
                                 * (_sigmoid(pre_x[rows, lanes]) * xc[rows, lanes]))
            hs, ps = [], []
            for r in range(SUBLANES):
                a = a_ref[j, pl.ds(r0 + r, SUBLANES, stride=SUBLANES), :]
                b = b_ref[j, pl.ds(r0 + r, SUBLANES, stride=SUBLANES), :]
                hs.append(b if r == 0 else a * hs[-1] + b)
                ps.append(a if r == 0 else a * ps[-1])
            p, q = ps[-1], hs[-1]
            for k in (1, 2, 4):
                p_sh = jnp.where(row >= k, pltpu.roll(p, k, 0), 1.0)
                q_sh = jnp.where(row >= k, pltpu.roll(q, k, 0), 0.0)
                q = p * q_sh + q
                p = p * p_sh
            seg_out = p * h_in[j] + q
            seg_in = jnp.where(row == 0, h_in[j], pltpu.roll(seg_out, 1, 0))
            for r in range(SUBLANES):
                h_ref[j, pl.ds(r0 + r, SUBLANES, stride=SUBLANES), :] = ps[r] * seg_in + hs[r]
            h_in[j] = jnp.broadcast_to(seg_out[SUBLANES - 1:, :], (SUBLANES, LANES))
            y = h_ref[j, rows, :] * _gelu_x2(gate_ref[rows, lanes])
            y_ref[slot, rows, lanes] = y.astype(y_ref.dtype)
            yield
    hprev_ref[...] = jnp.concatenate(h_in, axis=1)


def _ffn_init(carry_ref):
    carry_ref[...] = jnp.zeros_like(carry_ref)


def _ffn(res_ref, next_starts_seq, g_ref, wg_ref, wu_ref, cw_ref, cb_ref, wd_ref, hn_ref, act_ref,
         carry_ref, gwork_ref, side_work=None, side_plan=None, starts_seq=None):
    tm = res_ref.shape[0]
    for r0 in range(0, tm, tm // 2):
        rows = slice(r0, r0 + tm // 2)
        hn_ref[rows, :] = _rms(res_ref[rows, :], g_ref[...]).astype(BF16)

    if starts_seq is not None:
        @pl.when(starts_seq)
        def _():
            _ffn_init(carry_ref)

    for f in range(N_FF_CHUNKS):
        cols = slice(f * FF_CHUNK, (f + 1) * FF_CHUNK)
        hn = hn_ref[...]
        g = _dot(hn, wg_ref[:, cols])
        u = _dot(hn, wu_ref[:, cols])
        ys = []
        for j in range(FF_CHUNK // LANES):
            slab = f * (FF_CHUNK // LANES) + j
            work = slab % gwork_ref.shape[0]
            lanes = slice(cols.start + j * LANES, cols.start + (j + 1) * LANES)
            gj = g[:, j * LANES:(j + 1) * LANES]
            gwork_ref[work, :SUBLANES, :] = carry_ref[slab]
            gwork_ref[work, SUBLANES:, :] = gj
            yj = cb_ref[:, lanes] + cw_ref[FFN_CONV - 1:FFN_CONV, lanes] * gj
            for k in range(1, FFN_CONV):
                tap = gwork_ref[work, pl.ds(SUBLANES - k, tm, stride=1), :]
                yj = yj + cw_ref[FFN_CONV - 1 - k:FFN_CONV - k, lanes] * tap
            carry_ref[slab] = jnp.where(next_starts_seq, 0.0, gj[tm - SUBLANES:])
            ys.append(yj)
        y = jnp.concatenate(ys, axis=1)
        act_ref[:, cols] = (_gelu_x2(y) * u).astype(BF16)
        if side_work is not None:
            for _ in range(side_plan[f]):
                next(side_work, None)
    for c0 in range(0, D_MODEL, DOWN_COLS):
        cols = slice(c0, c0 + DOWN_COLS)
        res_ref[:, cols] = res_ref[:, cols] + _dot(act_ref[...], wd_ref[:, cols])
        if side_work is not None:
            for _ in range(side_plan[N_FF_CHUNKS + c0 // DOWN_COLS]):
                next(side_work, None)
    if side_work is not None:
        for _ in side_work:
            pass


def _ffn_scratch(tm):
    return [pltpu.VMEM((tm, D_MODEL), BF16), pltpu.VMEM((tm, D_FF), BF16),
            pltpu.VMEM((D_FF // LANES, SUBLANES, LANES), F32),
            pltpu.VMEM((2 * FF_CHUNK // LANES, SUBLANES + tm, LANES), F32)]


def _ab_out_ffn_kernel(h_ref, ya_ref, x0_ref, gate0_ref, xn_ref, gaten_ref,
                       lcw_ref, lcb_ref, wa_ref, ba_ref, wx_ref, bx_ref, lam_ref,
                       woa_ref, wob_ref, g_ref, wg_ref, wu_ref, cw_ref, cb_ref, wd_ref, o_ref,
                       ylru_ref, xprev_ref, hprev_ref, sa_ref, sb_ref, sh_ref,
                       hn_ref, act_ref, carry_ref, gwork_ref, *, tiles_per_seq):
    i = pl.program_id(0)
    slot = i % 2
    lru = (lcw_ref, lcb_ref, wa_ref, ba_ref, wx_ref, bx_ref, lam_ref, xprev_ref, hprev_ref,
           sa_ref, sb_ref, sh_ref)

    @pl.when(i == 0)
    def _():
        _ffn_init(carry_ref)
        for _ in _rglru_tile(x0_ref, gate0_ref, True, lru, ylru_ref, 0):
            pass

    next_starts_seq = (i + 1) % tiles_per_seq == 0
    side = _rglru_tile(xn_ref, gaten_ref, next_starts_seq, lru, ylru_ref, 1 - slot)
    next(side)
    half = o_ref.shape[0] // 2
    for r0 in (0, half):
        rows = slice(r0, r0 + half)
        o_ref[rows, :] = (h_ref[rows, :] + _dot(ya_ref[rows, :], woa_ref[...])
                          + _dot(ylru_ref[slot, rows, :], wob_ref[...]))
    _ffn(o_ref, next_starts_seq, g_ref, wg_ref, wu_ref, cw_ref, cb_ref, wd_ref,
         hn_ref, act_ref, carry_ref, gwork_ref, side_work=side, side_plan=LRU_SIDE_PLAN,
         starts_seq=i % tiles_per_seq == 0)


def _ab_out_ffn(h, ya, x_lru, gate_lru, lru_consts, woa, wob, ffn, layer, seq_len):
    n = h.shape[0]
    tm = ROW_TILE
    n_tiles = n // tm
    row = lambda c: pl.BlockSpec((tm, c), lambda i: (i, 0))
    first = pl.BlockSpec((tm, LRU_WIDTH), lambda i: (0, 0), pipeline_mode=pl.Buffered(1))
    ahead = pl.BlockSpec((tm, LRU_WIDTH), lambda i: (jnp.minimum(i + 1, n_tiles - 1), 0))
    consts = (*lru_consts, woa, wob, *ffn)
    return pl.pallas_call(
        functools.partial(_ab_out_ffn_kernel, tiles_per_seq=seq_len // tm),
        grid=(n_tiles,),
        in_specs=[row(D_MODEL), row(ya.shape[1]), first, first, ahead, ahead]
                 + [_const_spec(a.shape) for a in (*lru_consts, woa, wob)]
                 + [_layer_spec(a.shape, layer) for a in ffn],
        out_specs=row(D_MODEL),
        out_shape=jax.ShapeDtypeStruct((n, D_MODEL), F32),
        scratch_shapes=[pltpu.VMEM((2, tm, LRU_WIDTH), BF16),
                        pltpu.VMEM((LRU_WIDTH // LANES, SUBLANES + tm, LANES), F32),
                        pltpu.VMEM((SUBLANES, LRU_WIDTH), F32)]
                       + [pltpu.VMEM((LRU_WIDTH // LANES, tm, LANES), F32)] * 3 + _ffn_scratch(tm),
        compiler_params=pltpu.CompilerParams(dimension_semantics=("arbitrary",),
                                             vmem_limit_bytes=VMEM_LIMIT),
        name="ab_out_ffn",
    )(h, ya, x_lru, gate_lru, x_lru, gate_lru, *consts)


def _sgu_ffn_kernel(h_ref, cg_ref, win_ref, lng_ref, lnb_ref, ws_ref, bs_ref, wout_ref,
                    g_ref, wg_ref, wu_ref, cw_ref, cb_ref, wd_ref, fg_ref, o_ref,
                    u_ref, v_ref, gated_ref, hn_ref, act_ref, carry_ref, gwork_ref, *, tiles_per_seq):
    i = pl.program_id(0)

    @pl.when(i == 0)
    def _():
        _ffn_init(carry_ref)

    tiles = h_ref.shape[0] // ROW_TILE
    for sub in range(tiles):
        rows = pl.ds(sub * ROW_TILE, ROW_TILE)
        next_starts_seq = (i * tiles + sub + 1) % tiles_per_seq == 0
        _sgu_ffn_tile(h_ref.at[rows], cg_ref, win_ref, lng_ref, lnb_ref, ws_ref, bs_ref, wout_ref,
                      g_ref, wg_ref, wu_ref, cw_ref, cb_ref, wd_ref, fg_ref, o_ref.at[rows],
                      u_ref, v_ref, gated_ref, hn_ref, act_ref, carry_ref, gwork_ref, next_starts_seq)


def _sgu_ffn_tile(h_ref, cg_ref, win_ref, lng_ref, lnb_ref, ws_ref, bs_ref, wout_ref,
                  g_ref, wg_ref, wu_ref, cw_ref, cb_ref, wd_ref, fg_ref, o_ref,
                  u_ref, v_ref, gated_ref, hn_ref, act_ref, carry_ref, gwork_ref, next_starts_seq):
    tm = h_ref.shape[0]
    h = h_ref[...]
    xn = _rms(h, cg_ref[...]).astype(BF16)
    v = _gelu_x2(_dot(xn, win_ref[:, D_MODEL:]))
    mu = jnp.mean(v, axis=-1, keepdims=True)
    vc = v - mu
    var = jnp.mean(vc * vc, axis=-1, keepdims=True)
    v_ref[...] = (vc * lax.rsqrt(var + 4.0 * NORM_EPS) * lng_ref[...] + lnb_ref[...]).astype(BF16)
    u_ref[...] = _gelu_x2(_dot(xn, win_ref[:, :D_MODEL]))

    n_chunks = tm // CHUNK
    r = lax.broadcasted_iota(jnp.int32, (CHUNK, CHUNK), 0)
    c = lax.broadcasted_iota(jnp.int32, (CHUNK, CHUNK), 1)
    for gp in range(SGU_GROUPS):
        lanes = slice(gp * CHUNK, (gp + 1) * CHUNK)
        w = jnp.where(c <= r, ws_ref[gp], 0.0).astype(BF16)
        rhs = jnp.concatenate([v_ref[ck * CHUNK:(ck + 1) * CHUNK, lanes] for ck in range(n_chunks)],
                              axis=1)
        sg = _dot(w, rhs) + bs_ref[:, gp:gp + 1]
        for ck in range(n_chunks):
            rows = slice(ck * CHUNK, (ck + 1) * CHUNK)
            gated_ref[rows, lanes] = (u_ref[rows, lanes] * sg[:, ck * CHUNK:(ck + 1) * CHUNK]).astype(BF16)

    o_ref[...] = h_ref[...] + _dot(gated_ref[...], wout_ref[...])
    _ffn(o_ref, next_starts_seq, g_ref, wg_ref, wu_ref, cw_ref, cb_ref, wd_ref, hn_ref, act_ref,
         carry_ref, gwork_ref)
    o_ref[...] = _rms(o_ref[...], fg_ref[...])


def _sgu_ffn(h, cg, win, lng, lnb, ws, bs_t, wout, ffn, layer, fg, seq_len):
    n = h.shape[0]
    tm = ROW_TILE
    row = lambda c: pl.BlockSpec((SGU_STEP_ROWS, c), lambda i: (i, 0))
    sgu = (cg, win, lng, lnb, ws, bs_t, wout)
    consts = (*sgu, *ffn, fg)
    return pl.pallas_call(
        functools.partial(_sgu_ffn_kernel, tiles_per_seq=seq_len // tm),
        grid=(n // SGU_STEP_ROWS,),
        in_specs=[row(D_MODEL)] + [_const_spec(a.shape) for a in sgu]
                 + [_layer_spec(a.shape, layer) for a in ffn] + [_const_spec(fg.shape)],
        out_specs=row(D_MODEL),
        out_shape=jax.ShapeDtypeStruct((n, D_MODEL), F32),
        scratch_shapes=[pltpu.VMEM((tm, D_MODEL), F32), pltpu.VMEM((tm, D_MODEL), BF16),
                        pltpu.VMEM((tm, D_MODEL), BF16)] + _ffn_scratch(tm),
        compiler_params=pltpu.CompilerParams(dimension_semantics=("arbitrary",),
                                             vmem_limit_bytes=VMEM_LIMIT),
        name="sgu_ffn",
    )(h, *consts)


def _ffn_params(norm, w_gate, w_up, conv_w, conv_b, w_down):
    depth = norm.shape[0]
    return (norm.reshape(depth, 1, D_MODEL), w_gate.astype(BF16), (0.5 * w_up).astype(BF16),
            conv_w, conv_b.reshape(depth, 1, D_FF), w_down.astype(BF16))


def _block_diag(w):
    heads, blk, _ = w.shape
    eye = jnp.eye(heads, dtype=w.dtype)
    return (w[:, :, None, :] * eye[:, None, :, None]).reshape(heads * blk, heads * blk)


def kernel(x, positions, ab_norm, ab_w_in, ab_q_norm, ab_w_q_b, ab_kv_norm, ab_w_kv_b, ab_conv_w, ab_conv_b, ab_w_rg_a, ab_b_rg_a, ab_w_rg_x, ab_b_rg_x, ab_lambda, ab_w_out, c_norm, c_w_in, c_ln_g, c_ln_b, c_w_s, c_b_s, c_w_out, ffn_norm, ffn_w_gate, ffn_w_up, ffn_conv_w, ffn_conv_b, ffn_w_down, final_norm):
    b, s, d = x.shape
    n = b * s
    h = x.reshape(n, d)
    pos = positions.reshape(n, 1)

    w_in = ab_w_in[0]
    o2 = Q_LORA + KV_LORA
    o3 = o2 + QK_ROPE
    zeros = lambda c: jnp.zeros((d, c), w_in.dtype)
    w_in_p = jnp.concatenate([w_in[:, :o2], zeros(QK_NOPE), w_in[:, o2:o3],
                              zeros(HEAD_PAD - QK_NOPE - QK_ROPE), w_in[:, o3:]], axis=1).astype(BF16)
    qk = QK_NOPE + QK_ROPE
    half = QK_ROPE // 2
    wq3 = ab_w_q_b[0].reshape(Q_LORA, MLA_HEADS, qk)
    pad_q = lambda w: jnp.pad(w, ((0, 0), (0, 0), (0, HEAD_PAD - w.shape[-1]))
                              ).reshape(Q_LORA, MLA_HEADS * HEAD_PAD)
    wq_rot = jnp.concatenate([jnp.zeros_like(wq3[..., :QK_NOPE]), -wq3[..., QK_NOPE + half:],
                              wq3[..., QK_NOPE:QK_NOPE + half]], axis=-1)
    wq = jnp.concatenate([pad_q(wq3), pad_q(wq_rot)], axis=1).astype(BF16)
    wkv = ab_w_kv_b[0].reshape(KV_LORA, MLA_HEADS, QK_NOPE + V_HEAD)
    wk = jnp.pad(wkv[:, :, :QK_NOPE], ((0, 0), (0, 0), (0, HEAD_PAD - QK_NOPE))
                 ).reshape(KV_LORA, MLA_HEADS * HEAD_PAD).astype(BF16)
    wv = wkv[:, :, QK_NOPE:].reshape(KV_LORA, MLA_HEADS * V_HEAD).astype(BF16)
    freq = jnp.exp(-math.log(ROPE_BASE) * jnp.arange(half, dtype=F32) / half)
    invf = jnp.tile(freq, LANES // half).reshape(1, LANES)

    q, k, v, x_lru, gate_lru = _ab_in(
        h, pos, ab_norm[0].reshape(1, d), w_in_p, ab_q_norm[0].reshape(1, Q_LORA), wq,
        ab_kv_norm[0].reshape(1, KV_LORA), wk, wv, invf)

    y_mla = _attention(q.reshape(b, s, -1), k.reshape(b, s, -1), v.reshape(b, s, -1))
    lru_consts = (ab_conv_w[0], ab_conv_b[0].reshape(1, -1),
                  _block_diag(ab_w_rg_a[0]).astype(BF16), ab_b_rg_a[0].reshape(1, -1),
                  _block_diag(ab_w_rg_x[0]).astype(BF16), ab_b_rg_x[0].reshape(1, -1),
                  ab_lambda[0].reshape(1, -1))

    w_out = ab_w_out[0].astype(BF16)
    mla_w = MLA_HEADS * V_HEAD
    ffn = _ffn_params(ffn_norm, ffn_w_gate, ffn_w_up, ffn_conv_w, ffn_conv_b, ffn_w_down)
    h = _ab_out_ffn(h, y_mla.reshape(n, mla_w), x_lru, gate_lru, lru_consts,
                    w_out[:mla_w], 0.5 * w_out[mla_w:], ffn, 0, seq_len=s)

    out = _sgu_ffn(h, c_norm[0].reshape(1, d), c_w_in[0].astype(BF16), c_ln_g[0].reshape(1, -1),
                   c_ln_b[0].reshape(1, -1), 0.5 * c_w_s[0], 0.5 * c_b_s[0].T, c_w_out[0].astype(BF16),
                   ffn, 1, final_norm.reshape(1, d), seq_len=s)
    return out.reshape(b, s, d)
```

```python
import functools
import math

import jax
import jax.numpy as jnp
from jax import lax
from jax.experimental import pallas as pl
from jax.experimental.pallas import tpu as pltpu

F32 = jnp.float32
BF16 = jnp.bfloat16

D_MODEL = 1024
MLA_HEADS = 8
Q_LORA = 256
KV_LORA = 128
QK_NOPE = 64
QK_ROPE = 32
V_HEAD = 64
ROPE_BASE = 10000.0
LRU_WIDTH = 512
LRU_CONV = 4
LRU_C = 8.0
CHUNK = 128
SGU_GROUPS = 8
D_FF = 2816
FFN_CONV = 3
NORM_EPS = 1e-6

LANES = 128
SUBLANES = 8
MXU_TILE = 256
HEAD_PAD = 128

ROW_TILE = 512
AB_IN_TILE = 1024
AB_IN_SUB = 512
SGU_STEP_ROWS = 512
ATTN_TILE = 256
ATTN_PAIRS_PER_STEP = 1
ATTN_ROW_CHUNK = 128
LRU_SIDE_PLAN = (3,) * 10 + (2,) + (0,)
FF_CHUNK = 256
N_FF_CHUNKS = D_FF // FF_CHUNK
DOWN_COLS = 1024
VMEM_LIMIT = 56 * 1024 * 1024


def _dot(a, b):
    return jnp.dot(a, b, preferred_element_type=F32)


def _dot_nt(a, b):
    return lax.dot_general(a, b, (((1,), (1,)), ((), ())), preferred_element_type=F32)


def _gelu_x2(x):
    c = math.sqrt(2.0 / math.pi)
    t = jnp.tanh(x * (c + (c * 0.044715) * (x * x)))
    return x * t + x


def _sigmoid(x):
    return 0.5 * jnp.tanh(0.5 * x) + 0.5


def _expm1(x):
    u = jnp.exp(x)
    small = jnp.where(u == 1.0, x, (u - 1.0) * x / jnp.log(u))
    return jnp.where(x < -0.5, u - 1.0, small)


def _rms(x, g):
    ms = jnp.mean(x * x, axis=-1, keepdims=True)
    return x * lax.rsqrt(ms + NORM_EPS) * g


def _const_spec(shape):
    nd = len(shape)
    return pl.BlockSpec(shape, lambda *_: (0,) * nd, pipeline_mode=pl.Buffered(1))


def _layer_spec(shape, layer):
    nd = len(shape)
    return pl.BlockSpec((None,) + tuple(shape[1:]), lambda *_: (layer,) + (0,) * (nd - 1),
                        pipeline_mode=pl.Buffered(1))


def _ab_in_kernel(h_ref, pos_ref, g_ref, w_in_ref, qg_ref, wq_ref, kvg_ref, wk_ref, wv_ref,
                  invf_ref, q_ref, k_ref, v_ref, xl_ref, gate_ref, *, scale, sub):
    tm = h_ref.shape[0]
    o1 = Q_LORA
    o2 = o1 + KV_LORA
    o3 = o2 + HEAD_PAD
    o4 = o3 + LRU_WIDTH
    half = QK_ROPE // 2
    x1_lo, x2_lo, x2_hi = QK_NOPE, QK_NOPE + half, QK_NOPE + QK_ROPE

    def rope(blk, c, sd, su):
        return blk * c + pltpu.roll(blk, half, 1) * sd + pltpu.roll(blk, LANES - half, 1) * su

    def norm_stage(r):
        groups = LANES // QK_ROPE
        nb = sub // groups
        lane = lax.broadcasted_iota(jnp.int32, (nb, LANES), 1)
        pos_c = pos_ref[r.start + (groups - 1) * nb:r.start + groups * nb, :]
        for gi in range(groups - 2, -1, -1):
            pos_c = jnp.where(lane < (gi + 1) * QK_ROPE,
                              pos_ref[r.start + gi * nb:r.start + (gi + 1) * nb, :], pos_c)
        ang = pos_c.astype(F32) * invf_ref[...]
        cos_c = jnp.cos(ang)
        sin_c = jnp.sin(ang)
        c_tab, s_dn, s_up = [], [], []
        for gi in range(groups):
            shift = (x1_lo - gi * QK_ROPE) % LANES
            cosv = cos_c if shift == 0 else pltpu.roll(cos_c, shift, 1)
            sinv = sin_c if shift == 0 else pltpu.roll(sin_c, shift, 1)
            c_tab.append(jnp.where(lane < x1_lo, 1.0, jnp.where(lane < x2_hi, cosv, 0.0)))
            s_dn.append(jnp.where((lane >= x2_lo) & (lane < x2_hi), sinv, 0.0))
            s_up.append(jnp.where((lane >= x1_lo) & (lane < x2_lo), -sinv, 0.0))
        tabs = tuple(jnp.concatenate(t, axis=0) for t in (c_tab, s_dn, s_up))
        xn = _rms(h_ref[r, :], g_ref[...]).astype(BF16)
        return r, xn, tabs

    def proj_stage(r, xn, tabs):
        xl_ref[r, :] = _dot(xn, w_in_ref[:, o3:o4])
        gate_ref[r, :] = _dot(xn, w_in_ref[:, o4:])
        c_q = _dot(xn, w_in_ref[:, :o1])
        c_kv = _dot(xn, w_in_ref[:, o1:o2])
        kpe = _dot(xn, w_in_ref[:, o2:o3])
        return r, c_q, c_kv, kpe, tabs

    def latent_stage(r, c_q, c_kv, kpe, tabs):
        qn = _rms(c_q, qg_ref[...]).astype(BF16)
        width = MLA_HEADS * HEAD_PAD
        qf = _dot(qn, wq_ref[:, :width])
        qr = _dot(qn, wq_ref[:, width:])
        kvn = _rms(c_kv, kvg_ref[...]).astype(BF16)
        kf = _dot(kvn, wk_ref[...])
        v_ref[r, :] = _dot(kvn, wv_ref[...]).astype(BF16)
        return r, qf, qr, kf, kpe, tabs

    def rope_stage(r, qf, qr, kf, kpe, tabs):
        c_tab, s_dn, s_up = tabs
        cq, sq = c_tab * scale, (s_dn - s_up) * scale
        kpe_r = rope(kpe, c_tab, s_dn, s_up)
        for hd in range(MLA_HEADS):
            sl = slice(hd * HEAD_PAD, (hd + 1) * HEAD_PAD)
            q_ref[r, sl] = (qf[:, sl] * cq + qr[:, sl] * sq).astype(BF16)
            k_ref[r, sl] = (kf[:, sl] + kpe_r).astype(BF16)

    stages = (proj_stage, latent_stage, rope_stage)
    n_sub = tm // sub
    live = [None] * len(stages)
    for step in range(n_sub + len(stages)):
        nxt = [None] * len(stages)
        if step < n_sub:
            nxt[0] = norm_stage(slice(step * sub, (step + 1) * sub))
        for si, stage in enumerate(stages):
            if live[si] is not None:
                out = stage(*live[si])
                if si + 1 < len(stages):
                    nxt[si + 1] = out
        live = nxt


def _ab_in(h, pos, g, w_in, qg, wq, kvg, wk, wv, invf):
    n = h.shape[0]
    tm = AB_IN_TILE
    row = lambda c: pl.BlockSpec((tm, c), lambda i: (i, 0))
    scale = float((QK_NOPE + QK_ROPE) ** -0.5 * math.log2(math.e))
    return pl.pallas_call(
        functools.partial(_ab_in_kernel, scale=scale, sub=AB_IN_SUB),
        grid=(n // tm,),
        in_specs=[row(D_MODEL), row(1), _const_spec(g.shape), _const_spec(w_in.shape),
                  _const_spec(qg.shape), _const_spec(wq.shape), _const_spec(kvg.shape),
                  _const_spec(wk.shape), _const_spec(wv.shape), _const_spec(invf.shape)],
        out_specs=[row(MLA_HEADS * HEAD_PAD), row(MLA_HEADS * HEAD_PAD), row(MLA_HEADS * V_HEAD),
                   row(LRU_WIDTH), row(LRU_WIDTH)],
        out_shape=[jax.ShapeDtypeStruct((n, MLA_HEADS * HEAD_PAD), BF16),
                   jax.ShapeDtypeStruct((n, MLA_HEADS * HEAD_PAD), BF16),
                   jax.ShapeDtypeStruct((n, MLA_HEADS * V_HEAD), BF16),
                   jax.ShapeDtypeStruct((n, LRU_WIDTH), F32),
                   jax.ShapeDtypeStruct((n, LRU_WIDTH), F32)],
        compiler_params=pltpu.CompilerParams(dimension_semantics=("arbitrary",),
                                             vmem_limit_bytes=VMEM_LIMIT),
        name="ab_in",
    )(h, pos, g, w_in, qg, wq, kvg, wk, wv, invf)


def _attn_kernel(q_ref, k_ref, v_ref, o_ref, s_ref, p_ref):
    s_len = q_ref.shape[0]
    t = ATTN_TILE
    rc = ATTN_ROW_CHUNK
    r = lax.broadcasted_iota(jnp.int32, (rc, t), 0)
    c = lax.broadcasted_iota(jnp.int32, (rc, t), 1)
    lane = lax.broadcasted_iota(jnp.int32, (t, LANES), 1)
    n_tiles = s_len // t

    def scores(i, hd, slot):
        hl = slice(hd * HEAD_PAD, (hd + 1) * HEAD_PAD)
        kv = (i + 1) * t
        s_ref[slot, hd % 2, :, :kv] = _dot_nt(q_ref[i * t:(i + 1) * t, hl], k_ref[:kv, hl])

    def softmax(i, hd, slot):
        kv = (i + 1) * t
        sums = []
        for r0 in range(0, t, rc):
            rows = slice(r0, r0 + rc)
            s = s_ref[slot, hd, rows, :kv]
            diag = jnp.where(c <= r + r0, s[:, kv - t:], -jnp.inf)
            s = diag if i == 0 else jnp.concatenate([s[:, :kv - t], diag], axis=1)
            p = jnp.exp2(s - jnp.max(s, axis=1, keepdims=True))
            p_ref[slot, hd, rows, :kv] = p.astype(BF16)
            sums.append(jnp.sum(p, axis=1, keepdims=True))
        return jnp.concatenate(sums, axis=0)

    def values(i, pair, hd, slot, l):
        kv = (i + 1) * t
        return _dot(p_ref[slot, hd, :, :kv], v_ref[:kv, pair * LANES:(pair + 1) * LANES]) / l

    n_pairs = q_ref.shape[1] // (2 * HEAD_PAD)
    order = [(pair, i) for pair in range(n_pairs) for i in range(n_tiles - 1, -1, -1)]
    s_cur = p_cur = None
    for step in range(len(order) + 2):
        s_next = p_next = None
        if step < len(order):
            pair, i = order[step]
            for hd in range(2):
                scores(i, 2 * pair + hd, step % 2)
            s_next = (pair, i, step % 2)
        if s_cur is not None:
            pair, i, slot = s_cur
            p_next = ([softmax(i, hd, slot) for hd in range(2)], pair, i, slot)
        if p_cur is not None:
            ls, pair, i, slot = p_cur
            o0, o1 = (values(i, pair, hd, slot, ls[hd]) for hd in range(2))
            o_ref[i * t:(i + 1) * t, pair * LANES:(pair + 1) * LANES] = jnp.where(
                lane < V_HEAD, o0, o1).astype(o_ref.dtype)
        s_cur, p_cur = s_next, p_next


def _attention(q, k, v):
    b, s, _ = q.shape
    pp = ATTN_PAIRS_PER_STEP
    steps = MLA_HEADS // (2 * pp)
    return pl.pallas_call(
        _attn_kernel,
        grid=(b, steps),
        in_specs=[pl.BlockSpec((None, s, pp * 2 * HEAD_PAD), lambda bi, p: (bi, 0, p)),
                  pl.BlockSpec((None, s, pp * 2 * HEAD_PAD), lambda bi, p: (bi, 0, p)),
                  pl.BlockSpec((None, s, pp * 2 * V_HEAD), lambda bi, p: (bi, 0, p))],
        out_specs=pl.BlockSpec((None, s, pp * 2 * V_HEAD), lambda bi, p: (bi, 0, p)),
        out_shape=jax.ShapeDtypeStruct((b, s, MLA_HEADS * V_HEAD), BF16),
        scratch_shapes=[pltpu.VMEM((2, 2, ATTN_TILE, s), F32), pltpu.VMEM((2, 2, ATTN_TILE, s), BF16)],
        compiler_params=pltpu.CompilerParams(dimension_semantics=("arbitrary", "arbitrary"),
                                             vmem_limit_bytes=VMEM_LIMIT),
        name="mla_attn",
    )(q, k, v)


def _rglru_tile(x_ref, gate_ref, first, lru, y_ref, slot):
    (cw_ref, cb_ref, wa_ref, ba_ref, wx_ref, bx_ref, lam_ref,
     xprev_ref, hprev_ref, a_ref, b_ref, h_ref) = lru
    ts, c = x_ref.shape
    slabs = c // LANES
    if first is True:
        h_in = [jnp.zeros((SUBLANES, LANES), F32)] * slabs
    else:
        h_in = [jnp.where(first, 0.0, hprev_ref[:, j * LANES:(j + 1) * LANES]) for j in range(slabs)]
    xcs = []
    for j in range(slabs):
        lanes = slice(j * LANES, (j + 1) * LANES)
        xj = x_ref[:, lanes]
        if first is True:
            xprev_ref[j, :SUBLANES, :] = jnp.zeros((SUBLANES, LANES), F32)
        else:
            xprev_ref[j, :SUBLANES, :] = jnp.where(first, 0.0, xprev_ref[j, :SUBLANES, :])
        xprev_ref[j, SUBLANES:, :] = xj
        xcj = cb_ref[:, lanes] + cw_ref[LRU_CONV - 1:LRU_CONV, lanes] * xj
        for k in range(1, LRU_CONV):
            tap = xprev_ref[j, pl.ds(SUBLANES - k, ts, stride=1), :]
            xcj = xcj + cw_ref[LRU_CONV - 1 - k:LRU_CONV - k, lanes] * tap
        xprev_ref[j, :SUBLANES, :] = xj[ts - SUBLANES:]
        xcs.append(xcj)
    xc = jnp.concatenate(xcs, axis=1)
    xb = xc.astype(BF16)

    def gate_pre(w_ref, b_ref):
        parts = [_dot(xb[:, t0:t0 + MXU_TILE], w_ref[t0:t0 + MXU_TILE, t0:t0 + MXU_TILE])
                 for t0 in range(0, c, MXU_TILE)]
        return jnp.concatenate(parts, axis=1) + b_ref[...]

    pre_a = gate_pre(wa_ref, ba_ref)
    pre_x = gate_pre(wx_ref, bx_ref)
    z = -lam_ref[...]
    softplus = jnp.maximum(z, 0.0) + jnp.log1p(jnp.exp(-jnp.abs(z)))
    row = lax.broadcasted_iota(jnp.int32, (SUBLANES, LANES), 0)
    yield

    blk = SUBLANES * SUBLANES
    for r0 in range(0, ts, blk):
        rows = slice(r0, r0 + blk)
        for j in range(slabs):
            lanes = slice(j * LANES, (j + 1) * LANES)
            log_a = (-LRU_C) * _sigmoid(pre_a[rows, lanes]) * softplus[:, lanes]
            a_ref[j, rows, :] = jnp.exp(log_a)
            b_ref[j, rows, :] = (jnp.sqrt(-_expm1(2.0 * log_a))
                                 * (_sigmoid(pre_x[rows, lanes]) * xc[rows, lanes]))
            hs, ps = [], []
            for r in range(SUBLANES):
                a = a_ref[j, pl.ds(r0 + r, SUBLANES, stride=SUBLANES), :]
                b = b_ref[j, pl.ds(r0 + r, SUBLANES, stride=SUBLANES), :]
                hs.append(b if r == 0 else a * hs[-1] + b)
                ps.append(a if r == 0 else a * ps[-1])
            p, q = ps[-1], hs[-1]
            for k in (1, 2, 4):
                p_sh = jnp.where(row >= k, pltpu.roll(p, k, 0), 1.0)
                q_sh = jnp.where(row >= k, pltpu.roll(q, k, 0), 0.0)
                q = p * q_sh + q
                p = p * p_sh
            seg_out = p * h_in[j] + q
            seg_in = jnp.where(row == 0, h_in[j], pltpu.roll(seg_out, 1, 0))
            for r in range(SUBLANES):
                h_ref[j, pl.ds(r0 + r, SUBLANES, stride=SUBLANES), :] = ps[r] * seg_in + hs[r]
            h_in[j] = jnp.broadcast_to(seg_out[SUBLANES - 1:, :], (SUBLANES, LANES))
            y = h_ref[j, rows, :] * _gelu_x2(gate_ref[rows, lanes])
            y_ref[slot, rows, lanes] = y.astype(y_ref.dtype)
            yield
    hprev_ref[...] = jnp.concatenate(h_in, axis=1)


def _ffn_init(carry_ref):
    carry_ref[...] = jnp.zeros_like(carry_ref)


def _ffn(res_ref, next_starts_seq, g_ref, wg_ref, wu_ref, cw_ref, cb_ref, wd_ref, hn_ref, act_ref,
         carry_ref, gwork_ref, side_work=None, side_plan=None, starts_seq=None):
    tm = res_ref.shape[0]
    for r0 in range(0, tm, tm // 2):
        rows = slice(r0, r0 + tm // 2)
        hn_ref[rows, :] = _rms(res_ref[rows, :], g_ref[...]).astype(BF16)

    if starts_seq is not None:
        @pl.when(starts_seq)
        def _():
            _ffn_init(carry_ref)

    for f in range(N_FF_CHUNKS):
        cols = slice(f * FF_CHUNK, (f + 1) * FF_CHUNK)
        hn = hn_ref[...]
        g = _dot(hn, wg_ref[:, cols])
        u = _dot(hn, wu_ref[:, cols])
        ys = []
        for j in range(FF_CHUNK // LANES):
            slab = f * (FF_CHUNK // LANES) + j
            work = slab % gwork_ref.shape[0]
            lanes = slice(cols.start + j * LANES, cols.start + (j + 1) * LANES)
            gj = g[:, j * LANES:(j + 1) * LANES]
            gwork_ref[work, :SUBLANES, :] = carry_ref[slab]
            gwork_ref[work, SUBLANES:, :] = gj
            yj = cb_ref[:, lanes] + cw_ref[FFN_CONV - 1:FFN_CONV, lanes] * gj
            for k in range(1, FFN_CONV):
                tap = gwork_ref[work, pl.ds(SUBLANES - k, tm, stride=1), :]
                yj = yj + cw_ref[FFN_CONV - 1 - k:FFN_CONV - k, lanes] * tap
            carry_ref[slab] = jnp.where(next_starts_seq, 0.0, gj[tm - SUBLANES:])
            ys.append(yj)
        y = jnp.concatenate(ys, axis=1)
        act_ref[:, cols] = (_gelu_x2(y) * u).astype(BF16)
        if side_work is not None:
            for _ in range(side_plan[f]):
                next(side_work, None)
    for c0 in range(0, D_MODEL, DOWN_COLS):
        cols = slice(c0, c0 + DOWN_COLS)
        res_ref[:, cols] = res_ref[:, cols] + _dot(act_ref[...], wd_ref[:, cols])
        if side_work is not None:
            for _ in range(side_plan[N_FF_CHUNKS + c0 // DOWN_COLS]):
                next(side_work, None)
    if side_work is not None:
        for _ in side_work:
            pass


def _ffn_scratch(tm):
    return [pltpu.VMEM((tm, D_MODEL), BF16), pltpu.VMEM((tm, D_FF), BF16),
            pltpu.VMEM((D_FF // LANES, SUBLANES, LANES), F32),
            pltpu.VMEM((2 * FF_CHUNK // LANES, SUBLANES + tm, LANES), F32)]


def _ab_out_ffn_kernel(h_ref, ya_ref, x0_ref, gate0_ref, xn_ref, gaten_ref,
                       lcw_ref, lcb_ref, wa_ref, ba_ref, wx_ref, bx_ref, lam_ref,
                       woa_ref, wob_ref, g_ref, wg_ref, wu_ref, cw_ref, cb_ref, wd_ref, o_ref,
                       ylru_ref, xprev_ref, hprev_ref, sa_ref, sb_ref, sh_ref,
                       hn_ref, act_ref, carry_ref, gwork_ref, *, tiles_per_seq):
    i = pl.program_id(0)
    slot = i % 2
    lru = (lcw_ref, lcb_ref, wa_ref, ba_ref, wx_ref, bx_ref, lam_ref, xprev_ref, hprev_ref,
           sa_ref, sb_ref, sh_ref)

    @pl.when(i == 0)
    def _():
        _ffn_init(carry_ref)
        for _ in _rglru_tile(x0_ref, gate0_ref, True, lru, ylru_ref, 0):
            pass

    next_starts_seq = (i + 1) % tiles_per_seq == 0
    side = _rglru_tile(xn_ref, gaten_ref, next_starts_seq, lru, ylru_ref, 1 - slot)
    next(side)
    half = o_ref.shape[0] // 2
    for r0 in (0, half):
        rows = slice(r0, r0 + half)
        o_ref[rows, :] = (h_ref[rows, :] + _dot(ya_ref[rows, :], woa_ref[...])
                          + _dot(ylru_ref[slot, rows, :], wob_ref[...]))
    _ffn(o_ref, next_starts_seq, g_ref, wg_ref, wu_ref, cw_ref, cb_ref, wd_ref,
         hn_ref, act_ref, carry_ref, gwork_ref, side_work=side, side_plan=LRU_SIDE_PLAN,
         starts_seq=i % tiles_per_seq == 0)


def _ab_out_ffn(h, ya, x_lru, gate_lru, lru_consts, woa, wob, ffn, layer, seq_len):
    n = h.shape[0]
    tm = ROW_TILE
    n_tiles = n // tm
    row = lambda c: pl.BlockSpec((tm, c), lambda i: (i, 0))
    first = pl.BlockSpec((tm, LRU_WIDTH), lambda i: (0, 0), pipeline_mode=pl.Buffered(1))
    ahead = pl.BlockSpec((tm, LRU_WIDTH), lambda i: (jnp.minimum(i + 1, n_tiles - 1), 0))
    consts = (*lru_consts, woa, wob, *ffn)
    return pl.pallas_call(
        functools.partial(_ab_out_ffn_kernel, tiles_per_seq=seq_len // tm),
        grid=(n_tiles,),
        in_specs=[row(D_MODEL), row(ya.shape[1]), first, first, ahead, ahead]
                 + [_const_spec(a.shape) for a in (*lru_consts, woa, wob)]
                 + [_layer_spec(a.shape, layer) for a in ffn],
        out_specs=row(D_MODEL),
        out_shape=jax.ShapeDtypeStruct((n, D_MODEL), F32),
        scratch_shapes=[pltpu.VMEM((2, tm, LRU_WIDTH), BF16),
                        pltpu.VMEM((LRU_WIDTH // LANES, SUBLANES + tm, LANES), F32),
                        pltpu.VMEM((SUBLANES, LRU_WIDTH), F32)]
                       + [pltpu.VMEM((LRU_WIDTH // LANES, tm, LANES), F32)] * 3 + _ffn_scratch(tm),
        compiler_params=pltpu.CompilerParams(dimension_semantics=("arbitrary",),
                                             vmem_limit_bytes=VMEM_LIMIT),
        name="ab_out_ffn",
    )(h, ya, x_lru, gate_lru, x_lru, gate_lru, *consts)


def _sgu_ffn_kernel(h_ref, cg_ref, win_ref, lng_ref, lnb_ref, ws_ref, bs_ref, wout_ref,
                    g_ref, wg_ref, wu_ref, cw_ref, cb_ref, wd_ref, fg_ref, o_ref,
                    u_ref, v_ref, gated_ref, hn_ref, act_ref, carry_ref, gwork_ref, *, tiles_per_seq):
    i = pl.program_id(0)

    @pl.when(i == 0)
    def _():
        _ffn_init(carry_ref)

    tiles = h_ref.shape[0] // ROW_TILE
    for sub in range(tiles):
        rows = pl.ds(sub * ROW_TILE, ROW_TILE)
        next_starts_seq = (i * tiles + sub + 1) % tiles_per_seq == 0
        _sgu_ffn_tile(h_ref.at[rows], cg_ref, win_ref, lng_ref, lnb_ref, ws_ref, bs_ref, wout_ref,
                      g_ref, wg_ref, wu_ref, cw_ref, cb_ref, wd_ref, fg_ref, o_ref.at[rows],
                      u_ref, v_ref, gated_ref, hn_ref, act_ref, carry_ref, gwork_ref, next_starts_seq)


def _sgu_ffn_tile(h_ref, cg_ref, win_ref, lng_ref, lnb_ref, ws_ref, bs_ref, wout_ref,
                  g_ref, wg_ref, wu_ref, cw_ref, cb_ref, wd_ref, fg_ref, o_ref,
                  u_ref, v_ref, gated_ref, hn_ref, act_ref, carry_ref, gwork_ref, next_starts_seq):
    tm = h_ref.shape[0]
    h = h_ref[...]
    xn = _rms(h, cg_ref[...]).astype(BF16)
    v = _gelu_x2(_dot(xn, win_ref[:, D_MODEL:]))
    mu = jnp.mean(v, axis=-1, keepdims=True)
    vc = v - mu
    var = jnp.mean(vc * vc, axis=-1, keepdims=True)
    v_ref[...] = (vc * lax.rsqrt(var + 4.0 * NORM_EPS) * lng_ref[...] + lnb_ref[...]).astype(BF16)
    u_ref[...] = _gelu_x2(_dot(xn, win_ref[:, :D_MODEL]))

    n_chunks = tm // CHUNK
    r = lax.broadcasted_iota(jnp.int32, (CHUNK, CHUNK), 0)
    c = lax.broadcasted_iota(jnp.int32, (CHUNK, CHUNK), 1)
    for gp in range(SGU_GROUPS):
        lanes = slice(gp * CHUNK, (gp + 1) * CHUNK)
        w = jnp.where(c <= r, ws_ref[gp], 0.0).astype(BF16)
        rhs = jnp.concatenate([v_ref[ck * CHUNK:(ck + 1) * CHUNK, lanes] for ck in range(n_chunks)],
                              axis=1)
        sg = _dot(w, rhs) + bs_ref[:, gp:gp + 1]
        for ck in range(n_chunks):
            rows = slice(ck * CHUNK, (ck + 1) * CHUNK)
            gated_ref[rows, lanes] = (u_ref[rows, lanes] * sg[:, ck * CHUNK:(ck + 1) * CHUNK]).astype(BF16)

    o_ref[...] = h_ref[...] + _dot(gated_ref[...], wout_ref[...])
    _ffn(o_ref, next_starts_seq, g_ref, wg_ref, wu_ref, cw_ref, cb_ref, wd_ref, hn_ref, act_ref,
         carry_ref, gwork_ref)
    o_ref[...] = _rms(o_ref[...], fg_ref[...])


def _sgu_ffn(h, cg, win, lng, lnb, ws, bs_t, wout, ffn, layer, fg, seq_len):
    n = h.shape[0]
    tm = ROW_TILE
    row = lambda c: pl.BlockSpec((SGU_STEP_ROWS, c), lambda i: (i, 0))
    sgu = (cg, win, lng, lnb, ws, bs_t, wout)
    consts = (*sgu, *ffn, fg)
    return pl.pallas_call(
        functools.partial(_sgu_ffn_kernel, tiles_per_seq=seq_len // tm),
        grid=(n // SGU_STEP_ROWS,),
        in_specs=[row(D_MODEL)] + [_const_spec(a.shape) for a in sgu]
                 + [_layer_spec(a.shape, layer) for a in ffn] + [_const_spec(fg.shape)],
        out_specs=row(D_MODEL),
        out_shape=jax.ShapeDtypeStruct((n, D_MODEL), F32),
        scratch_shapes=[pltpu.VMEM((tm, D_MODEL), F32), pltpu.VMEM((tm, D_MODEL), BF16),
                        pltpu.VMEM((tm, D_MODEL), BF16)] + _ffn_scratch(tm),
        compiler_params=pltpu.CompilerParams(dimension_semantics=("arbitrary",),
                                             vmem_limit_bytes=VMEM_LIMIT),
        name="sgu_ffn",
    )(h, *consts)


def _ffn_params(norm, w_gate, w_up, conv_w, conv_b, w_down):
    depth = norm.shape[0]
    return (norm.reshape(depth, 1, D_MODEL), w_gate.astype(BF16), (0.5 * w_up).astype(BF16),
            conv_w, conv_b.reshape(depth, 1, D_FF), w_down.astype(BF16))


def _block_diag(w):
    heads, blk, _ = w.shape
    eye = jnp.eye(heads, dtype=w.dtype)
    return (w[:, :, None, :] * eye[:, None, :, None]).reshape(heads * blk, heads * blk)


def kernel(x, positions, ab_norm, ab_w_in, ab_q_norm, ab_w_q_b, ab_kv_norm, ab_w_kv_b, ab_conv_w, ab_conv_b, ab_w_rg_a, ab_b_rg_a, ab_w_rg_x, ab_b_rg_x, ab_lambda, ab_w_out, c_norm, c_w_in, c_ln_g, c_ln_b, c_w_s, c_b_s, c_w_out, ffn_norm, ffn_w_gate, ffn_w_up, ffn_conv_w, ffn_conv_b, ffn_w_down, final_norm):
    b, s, d = x.shape
    n = b * s
    h = x.reshape(n, d)
    pos = positions.reshape(n, 1)

    w_in = ab_w_in[0]
    o2 = Q_LORA + KV_LORA
    o3 = o2 + QK_ROPE
    zeros = lambda c: jnp.zeros((d, c), w_in.dtype)
    w_in_p = jnp.concatenate([w_in[:, :o2], zeros(QK_NOPE), w_in[:, o2:o3],
                              zeros(HEAD_PAD - QK_NOPE - QK_ROPE), w_in[:, o3:]], axis=1).astype(BF16)
    qk = QK_NOPE + QK_ROPE
    half = QK_ROPE // 2
    wq3 = ab_w_q_b[0].reshape(Q_LORA, MLA_HEADS, qk)
    pad_q = lambda w: jnp.pad(w, ((0, 0), (0, 0), (0, HEAD_PAD - w.shape[-1]))
                              ).reshape(Q_LORA, MLA_HEADS * HEAD_PAD)
    wq_rot = jnp.concatenate([jnp.zeros_like(wq3[..., :QK_NOPE]), -wq3[..., QK_NOPE + half:],
                              wq3[..., QK_NOPE:QK_NOPE + half]], axis=-1)
    wq = jnp.concatenate([pad_q(wq3), pad_q(wq_rot)], axis=1).astype(BF16)
    wkv = ab_w_kv_b[0].reshape(KV_LORA, MLA_HEADS, QK_NOPE + V_HEAD)
    wk = jnp.pad(wkv[:, :, :QK_NOPE], ((0, 0), (0, 0), (0, HEAD_PAD - QK_NOPE))
                 ).reshape(KV_LORA, MLA_HEADS * HEAD_PAD).astype(BF16)
    wv = wkv[:, :, QK_NOPE:].reshape(KV_LORA, MLA_HEADS * V_HEAD).astype(BF16)
    freq = jnp.exp(-math.log(ROPE_BASE) * jnp.arange(half, dtype=F32) / half)
    invf = jnp.tile(freq, LANES // half).reshape(1, LANES)

    q, k, v, x_lru, gate_lru = _ab_in(
        h, pos, ab_norm[0].reshape(1, d), w_in_p, ab_q_norm[0].reshape(1, Q_LORA), wq,
        ab_kv_norm[0].reshape(1, KV_LORA), wk, wv, invf)

    y_mla = _attention(q.reshape(b, s, -1), k.reshape(b, s, -1), v.reshape(b, s, -1))
    lru_consts = (ab_conv_w[0], ab_conv_b[0].reshape(1, -1),
                  _block_diag(ab_w_rg_a[0]).astype(BF16), ab_b_rg_a[0].reshape(1, -1),
                  _block_diag(ab_w_rg_x[0]).astype(BF16), ab_b_rg_x[0].reshape(1, -1),
                  ab_lambda[0].reshape(1, -1))

    w_out = ab_w_out[0].astype(BF16)
    mla_w = MLA_HEADS * V_HEAD
    ffn = _ffn_params(ffn_norm, ffn_w_gate, ffn_w_up, ffn_conv_w, ffn_conv_b, ffn_w_down)
    h = _ab_out_ffn(h, y_mla.reshape(n, mla_w), x_lru, gate_lru, lru_consts,
                    w_out[:mla_w], 0.5 * w_out[mla_w:], ffn, 0, seq_len=s)

    out = _sgu_ffn(h, c_norm[0].reshape(1, d), c_w_in[0].astype(BF16), c_ln_g[0].reshape(1, -1),
                   c_ln_b[0].reshape(1, -1), 0.5 * c_w_s[0], 0.5 * c_b_s[0].T, c_w_out[0].astype(BF16),
                   ffn, 1, final_norm.reshape(1, d), seq_len=s)
    return out.reshape(b, s, d)
```

```python
import functools
import math

import jax
import jax.numpy as jnp
from jax import lax
from jax.experimental import pallas as pl
from jax.experimental.pallas import tpu as pltpu

F32 = jnp.float32
BF16 = jnp.bfloat16

D_MODEL = 1024
MLA_HEADS = 8
Q_LORA = 256
KV_LORA = 128
QK_NOPE = 64
QK_ROPE = 32
V_HEAD = 64
ROPE_BASE = 10000.0
LRU_WIDTH = 512
LRU_CONV = 4
LRU_C = 8.0
CHUNK = 128
SGU_GROUPS = 8
D_FF = 2816
FFN_CONV = 3
NORM_EPS = 1e-6

LANES = 128
SUBLANES = 8
MXU_TILE = 256
HEAD_PAD = 128

ROW_TILE = 512
AB_IN_TILE = 1024
AB_IN_SUB = 512
SGU_STEP_ROWS = 512
ATTN_TILE = 256
ATTN_PAIRS_PER_STEP = 1
LRU_SIDE_PLAN = (3,) * 10 + (2,) + (0,)
FF_CHUNK = 256
N_FF_CHUNKS = D_FF // FF_CHUNK
DOWN_COLS = 1024
VMEM_LIMIT = 56 * 1024 * 1024


def _dot(a, b):
    return jnp.dot(a, b, preferred_element_type=F32)


def _dot_nt(a, b):
    return lax.dot_general(a, b, (((1,), (1,)), ((), ())), preferred_element_type=F32)


def _gelu_x2(x):
    c = math.sqrt(2.0 / math.pi)
    t = jnp.tanh(x * (c + (c * 0.044715) * (x * x)))
    return x * t + x


def _sigmoid(x):
    return 0.5 * jnp.tanh(0.5 * x) + 0.5


def _expm1(x):
    u = jnp.exp(x)
    small = jnp.where(u == 1.0, x, (u - 1.0) * x / jnp.log(u))
    return jnp.where(x < -0.5, u - 1.0, small)


def _zero_after(x):
    bits = lax.bitcast_convert_type(x, jnp.uint32)
    bits = lax.shift_right_logical(lax.shift_right_logical(bits, jnp.uint32(16)), jnp.uint32(16))
    return lax.bitcast_convert_type(bits, F32)


def _rms(x, g):
    ms = jnp.mean(x * x, axis=-1, keepdims=True)
    return x * lax.rsqrt(ms + NORM_EPS) * g


def _const_spec(shape):
    nd = len(shape)
    return pl.BlockSpec(shape, lambda *_: (0,) * nd, pipeline_mode=pl.Buffered(1))


def _layer_spec(shape, layer):
    nd = len(shape)
    return pl.BlockSpec((None,) + tuple(shape[1:]), lambda *_: (layer,) + (0,) * (nd - 1),
                        pipeline_mode=pl.Buffered(1))


def _ab_in_kernel(h_ref, pos_ref, g_ref, w_in_ref, qg_ref, wq_ref, kvg_ref, wk_ref, wv_ref,
                  invf_ref, q_ref, k_ref, v_ref, xl_ref, gate_ref, *, scale, sub):
    tm = h_ref.shape[0]
    o1 = Q_LORA
    o2 = o1 + KV_LORA
    o3 = o2 + HEAD_PAD
    o4 = o3 + LRU_WIDTH
    half = QK_ROPE // 2
    x1_lo, x2_lo, x2_hi = QK_NOPE, QK_NOPE + half, QK_NOPE + QK_ROPE

    def rope(blk, c, sd, su):
        return blk * c + pltpu.roll(blk, half, 1) * sd + pltpu.roll(blk, LANES - half, 1) * su

    def norm_stage(r):
        groups = LANES // QK_ROPE
        nb = sub // groups
        lane = lax.broadcasted_iota(jnp.int32, (nb, LANES), 1)
        pos_c = pos_ref[r.start + (groups - 1) * nb:r.start + groups * nb, :]
        for gi in range(groups - 2, -1, -1):
            pos_c = jnp.where(lane < (gi + 1) * QK_ROPE,
                              pos_ref[r.start + gi * nb:r.start + (gi + 1) * nb, :], pos_c)
        ang = pos_c.astype(F32) * invf_ref[...]
        cos_c = jnp.cos(ang)
        sin_c = jnp.sin(ang)
        c_tab, s_dn, s_up = [], [], []
        for gi in range(groups):
            shift = (x1_lo - gi * QK_ROPE) % LANES
            cosv = cos_c if shift == 0 else pltpu.roll(cos_c, shift, 1)
            sinv = sin_c if shift == 0 else pltpu.roll(sin_c, shift, 1)
            c_tab.append(jnp.where(lane < x1_lo, 1.0, jnp.where(lane < x2_hi, cosv, 0.0)))
            s_dn.append(jnp.where((lane >= x2_lo) & (lane < x2_hi), sinv, 0.0))
            s_up.append(jnp.where((lane >= x1_lo) & (lane < x2_lo), -sinv, 0.0))
        tabs = tuple(jnp.concatenate(t, axis=0) for t in (c_tab, s_dn, s_up))
        xn = _rms(h_ref[r, :], g_ref[...]).astype(BF16)
        return r, xn, tabs

    def proj_stage(r, xn, tabs):
        xl_ref[r, :] = _dot(xn, w_in_ref[:, o3:o4])
        gate_ref[r, :] = _dot(xn, w_in_ref[:, o4:])
        c_q = _dot(xn, w_in_ref[:, :o1])
        c_kv = _dot(xn, w_in_ref[:, o1:o2])
        kpe = _dot(xn, w_in_ref[:, o2:o3])
        return r, c_q, c_kv, kpe, tabs

    def latent_stage(r, c_q, c_kv, kpe, tabs):
        qn = _rms(c_q, qg_ref[...]).astype(BF16)
        width = MLA_HEADS * HEAD_PAD
        qf = _dot(qn, wq_ref[:, :width])
        qr = _dot(qn, wq_ref[:, width:])
        kvn = _rms(c_kv, kvg_ref[...]).astype(BF16)
        kf = _dot(kvn, wk_ref[...])
        v_ref[r, :] = _dot(kvn, wv_ref[...]).astype(BF16)
        return r, qf, qr, kf, kpe, tabs

    def rope_stage(r, qf, qr, kf, kpe, tabs):
        c_tab, s_dn, s_up = tabs
        cq, sq = c_tab * scale, (s_dn - s_up) * scale
        kpe_r = rope(kpe, c_tab, s_dn, s_up)
        for hd in range(MLA_HEADS):
            sl = slice(hd * HEAD_PAD, (hd + 1) * HEAD_PAD)
            q_ref[r, sl] = (qf[:, sl] * cq + qr[:, sl] * sq).astype(BF16)
            k_ref[r, sl] = (kf[:, sl] + kpe_r).astype(BF16)

    stages = (proj_stage, latent_stage, rope_stage)
    n_sub = tm // sub
    live = [None] * len(stages)
    for step in range(n_sub + len(stages)):
        nxt = [None] * len(stages)
        if step < n_sub:
            nxt[0] = norm_stage(slice(step * sub, (step + 1) * sub))
        for si, stage in enumerate(stages):
            if live[si] is not None:
                out = stage(*live[si])
                if si + 1 < len(stages):
                    nxt[si + 1] = out
        live = nxt


def _ab_in(h, pos, g, w_in, qg, wq, kvg, wk, wv, invf):
    n = h.shape[0]
    tm = AB_IN_TILE
    row = lambda c: pl.BlockSpec((tm, c), lambda i: (i, 0))
    scale = float((QK_NOPE + QK_ROPE) ** -0.5 * math.log2(math.e))
    return pl.pallas_call(
        functools.partial(_ab_in_kernel, scale=scale, sub=AB_IN_SUB),
        grid=(n // tm,),
        in_specs=[row(D_MODEL), row(1), _const_spec(g.shape), _const_spec(w_in.shape),
                  _const_spec(qg.shape), _const_spec(wq.shape), _const_spec(kvg.shape),
                  _const_spec(wk.shape), _const_spec(wv.shape), _const_spec(invf.shape)],
        out_specs=[row(MLA_HEADS * HEAD_PAD), row(MLA_HEADS * HEAD_PAD), row(MLA_HEADS * V_HEAD),
                   row(LRU_WIDTH), row(LRU_WIDTH)],
        out_shape=[jax.ShapeDtypeStruct((n, MLA_HEADS * HEAD_PAD), BF16),
                   jax.ShapeDtypeStruct((n, MLA_HEADS * HEAD_PAD), BF16),
                   jax.ShapeDtypeStruct((n, MLA_HEADS * V_HEAD), BF16),
                   jax.ShapeDtypeStruct((n, LRU_WIDTH), F32),
                   jax.ShapeDtypeStruct((n, LRU_WIDTH), F32)],
        compiler_params=pltpu.CompilerParams(dimension_semantics=("arbitrary",),
                                             vmem_limit_bytes=VMEM_LIMIT),
        name="ab_in",
    )(h, pos, g, w_in, qg, wq, kvg, wk, wv, invf)


def _attn_kernel(q_ref, k_ref, v_ref, o_ref):
    s_len = q_ref.shape[0]
    t = ATTN_TILE
    r = lax.broadcasted_iota(jnp.int32, (t, t), 0)
    c = lax.broadcasted_iota(jnp.int32, (t, t), 1)
    causal = c <= r
    lane = lax.broadcasted_iota(jnp.int32, (t, LANES), 1)
    n_tiles = s_len // t

    def scores(i, hd):
        hl = slice(hd * HEAD_PAD, (hd + 1) * HEAD_PAD)
        return _dot_nt(q_ref[i * t:(i + 1) * t, hl], k_ref[:(i + 1) * t, hl])

    def softmax(i, s):
        kv = (i + 1) * t
        diag = jnp.where(causal, s[:, kv - t:], -jnp.inf)
        s = diag if i == 0 else jnp.concatenate([s[:, :kv - t], diag], axis=1)
        p = jnp.exp2(s - jnp.max(s, axis=1, keepdims=True))
        return p.astype(BF16), jnp.sum(p, axis=1, keepdims=True)

    def values(i, pair, p, l):
        return _dot(p, v_ref[:(i + 1) * t, pair * LANES:(pair + 1) * LANES]) / l

    n_pairs = q_ref.shape[1] // (2 * HEAD_PAD)
    order = [(pair, i) for pair in range(n_pairs) for i in range(n_tiles - 1, -1, -1)]
    s_cur = p_cur = None
    for step in range(len(order) + 2):
        s_next = p_next = None
        if step < len(order):
            pair, i = order[step]
            s_next = ([scores(i, 2 * pair + hd) for hd in range(2)], pair, i)
        if s_cur is not None:
            p_next = ([softmax(s_cur[2], s) for s in s_cur[0]], s_cur[1], s_cur[2])
        if p_cur is not None:
            _, pair, i = p_cur
            o0, o1 = (values(i, pair, p, l) for p, l in p_cur[0])
            o_ref[i * t:(i + 1) * t, pair * LANES:(pair + 1) * LANES] = jnp.where(
                lane < V_HEAD, o0, o1).astype(o_ref.dtype)
        s_cur, p_cur = s_next, p_next


def _attention(q, k, v):
    b, s, _ = q.shape
    pp = ATTN_PAIRS_PER_STEP
    steps = MLA_HEADS // (2 * pp)
    return pl.pallas_call(
        _attn_kernel,
        grid=(b, steps),
        in_specs=[pl.BlockSpec((None, s, pp * 2 * HEAD_PAD), lambda bi, p: (bi, 0, p)),
                  pl.BlockSpec((None, s, pp * 2 * HEAD_PAD), lambda bi, p: (bi, 0, p)),
                  pl.BlockSpec((None, s, pp * 2 * V_HEAD), lambda bi, p: (bi, 0, p))],
        out_specs=pl.BlockSpec((None, s, pp * 2 * V_HEAD), lambda bi, p: (bi, 0, p)),
        out_shape=jax.ShapeDtypeStruct((b, s, MLA_HEADS * V_HEAD), BF16),
        compiler_params=pltpu.CompilerParams(dimension_semantics=("arbitrary", "arbitrary"),
                                             vmem_limit_bytes=VMEM_LIMIT),
        name="mla_attn",
    )(q, k, v)


def _rglru_tile(x_ref, gate_ref, first, lru, y_ref, slot, after=None):
    (cw_ref, cb_ref, wa_ref, ba_ref, wx_ref, bx_ref, lam_ref,
     xprev_ref, hprev_ref, a_ref, b_ref, h_ref) = lru
    ts, c = x_ref.shape
    slabs = c // LANES
    if first is True:
        h_in = [jnp.zeros((SUBLANES, LANES), F32)] * slabs
    else:
        h_in = [jnp.where(first, 0.0, hprev_ref[:, j * LANES:(j + 1) * LANES]) for j in range(slabs)]
    xcs = []
    for j in range(slabs):
        lanes = slice(j * LANES, (j + 1) * LANES)
        xj = x_ref[:, lanes]
        if first is True:
            xprev_ref[j, :SUBLANES, :] = jnp.zeros((SUBLANES, LANES), F32)
        else:
            xprev_ref[j, :SUBLANES, :] = jnp.where(first, 0.0, xprev_ref[j, :SUBLANES, :])
        xprev_ref[j, SUBLANES:, :] = xj
        xcj = cb_ref[:, lanes] + cw_ref[LRU_CONV - 1:LRU_CONV, lanes] * xj
        for k in range(1, LRU_CONV):
            tap = xprev_ref[j, pl.ds(SUBLANES - k, ts, stride=1), :]
            xcj = xcj + cw_ref[LRU_CONV - 1 - k:LRU_CONV - k, lanes] * tap
        xprev_ref[j, :SUBLANES, :] = xj[ts - SUBLANES:]
        xcs.append(xcj)
    xc = jnp.concatenate(xcs, axis=1)
    xb = xc.astype(BF16)

    def gate_pre(w_ref, b_ref):
        parts = [_dot(xb[:, t0:t0 + MXU_TILE], w_ref[t0:t0 + MXU_TILE, t0:t0 + MXU_TILE])
                 for t0 in range(0, c, MXU_TILE)]
        return jnp.concatenate(parts, axis=1) + b_ref[...]

    pre_a = gate_pre(wa_ref, ba_ref)
    pre_x = gate_pre(wx_ref, bx_ref)
    z = -lam_ref[...]
    softplus = jnp.maximum(z, 0.0) + jnp.log1p(jnp.exp(-jnp.abs(z)))
    row = lax.broadcasted_iota(jnp.int32, (SUBLANES, LANES), 0)
    yield

    blk = SUBLANES * SUBLANES
    for r0 in range(0, ts, blk):
        rows = slice(r0, r0 + blk)
        for j in range(slabs):
            lanes = slice(j * LANES, (j + 1) * LANES)
            pa = pre_a[rows, lanes]
            if after is not None and after[0] is not None:
                pa = pa + jnp.tile(_zero_after(after[0]), (blk // SUBLANES, 1))
            log_a = (-LRU_C) * _sigmoid(pa) * softplus[:, lanes]
            a_ref[j, rows, :] = jnp.exp(log_a)
            b_ref[j, rows, :] = (jnp.sqrt(-_expm1(2.0 * log_a))
                                 * (_sigmoid(pre_x[rows, lanes]) * xc[rows, lanes]))
            hs, ps = [], []
            for r in range(SUBLANES):
                a = a_ref[j, pl.ds(r0 + r, SUBLANES, stride=SUBLANES), :]
                b = b_ref[j, pl.ds(r0 + r, SUBLANES, stride=SUBLANES), :]
                hs.append(b if r == 0 else a * hs[-1] + b)
                ps.append(a if r == 0 else a * ps[-1])
            p, q = ps[-1], hs[-1]
            for k in (1, 2, 4):
                p_sh = jnp.where(row >= k, pltpu.roll(p, k, 0), 1.0)
                q_sh = jnp.where(row >= k, pltpu.roll(q, k, 0), 0.0)
                q = p * q_sh + q
                p = p * p_sh
            seg_out = p * h_in[j] + q
            seg_in = jnp.where(row == 0, h_in[j], pltpu.roll(seg_out, 1, 0))
            for r in range(SUBLANES):
                h_ref[j, pl.ds(r0 + r, SUBLANES, stride=SUBLANES), :] = ps[r] * seg_in + hs[r]
            h_in[j] = jnp.broadcast_to(seg_out[SUBLANES - 1:, :], (SUBLANES, LANES))
            y = h_ref[j, rows, :] * _gelu_x2(gate_ref[rows, lanes])
            y_ref[slot, rows, lanes] = y.astype(y_ref.dtype)
            yield
    hprev_ref[...] = jnp.concatenate(h_in, axis=1)


def _ffn_init(carry_ref):
    carry_ref[...] = jnp.zeros_like(carry_ref)


def _ffn(res_ref, next_starts_seq, g_ref, wg_ref, wu_ref, cw_ref, cb_ref, wd_ref, hn_ref, act_ref,
         carry_ref, gwork_ref, side_work=None, side_plan=None, starts_seq=None, side_after=None):
    tm = res_ref.shape[0]
    for r0 in range(0, tm, tm // 2):
        rows = slice(r0, r0 + tm // 2)
        hn_ref[rows, :] = _rms(res_ref[rows, :], g_ref[...]).astype(BF16)

    if starts_seq is not None:
        @pl.when(starts_seq)
        def _():
            _ffn_init(carry_ref)

    for f in range(N_FF_CHUNKS):
        cols = slice(f * FF_CHUNK, (f + 1) * FF_CHUNK)
        hn = hn_ref[...]
        g = _dot(hn, wg_ref[:, cols])
        u = _dot(hn, wu_ref[:, cols])
        ys = []
        for j in range(FF_CHUNK // LANES):
            slab = f * (FF_CHUNK // LANES) + j
            work = slab % gwork_ref.shape[0]
            lanes = slice(cols.start + j * LANES, cols.start + (j + 1) * LANES)
            gj = g[:, j * LANES:(j + 1) * LANES]
            gwork_ref[work, :SUBLANES, :] = carry_ref[slab]
            gwork_ref[work, SUBLANES:, :] = gj
            yj = cb_ref[:, lanes] + cw_ref[FFN_CONV - 1:FFN_CONV, lanes] * gj
            for k in range(1, FFN_CONV):
                tap = gwork_ref[work, pl.ds(SUBLANES - k, tm, stride=1), :]
                yj = yj + cw_ref[FFN_CONV - 1 - k:FFN_CONV - k, lanes] * tap
            carry_ref[slab] = jnp.where(next_starts_seq, 0.0, gj[tm - SUBLANES:])
            ys.append(yj)
        y = jnp.concatenate(ys, axis=1)
        act_ref[:, cols] = (_gelu_x2(y) * u).astype(BF16)
        if side_work is not None:
            if side_after is not None:
                side_after[0] = u[tm - SUBLANES:, FF_CHUNK - LANES:]
            for _ in range(side_plan[f]):
                next(side_work, None)
    for c0 in range(0, D_MODEL, DOWN_COLS):
        cols = slice(c0, c0 + DOWN_COLS)
        res_ref[:, cols] = res_ref[:, cols] + _dot(act_ref[...], wd_ref[:, cols])
        if side_work is not None:
            for _ in range(side_plan[N_FF_CHUNKS + c0 // DOWN_COLS]):
                next(side_work, None)
    if side_work is not None:
        for _ in side_work:
            pass


def _ffn_scratch(tm):
    return [pltpu.VMEM((tm, D_MODEL), BF16), pltpu.VMEM((tm, D_FF), BF16),
            pltpu.VMEM((D_FF // LANES, SUBLANES, LANES), F32),
            pltpu.VMEM((2 * FF_CHUNK // LANES, SUBLANES + tm, LANES), F32)]


def _ab_out_ffn_kernel(h_ref, ya_ref, x0_ref, gate0_ref, xn_ref, gaten_ref,
                       lcw_ref, lcb_ref, wa_ref, ba_ref, wx_ref, bx_ref, lam_ref,
                       woa_ref, wob_ref, g_ref, wg_ref, wu_ref, cw_ref, cb_ref, wd_ref, o_ref,
                       ylru_ref, xprev_ref, hprev_ref, sa_ref, sb_ref, sh_ref,
                       hn_ref, act_ref, carry_ref, gwork_ref, *, tiles_per_seq):
    i = pl.program_id(0)
    slot = i % 2
    lru = (lcw_ref, lcb_ref, wa_ref, ba_ref, wx_ref, bx_ref, lam_ref, xprev_ref, hprev_ref,
           sa_ref, sb_ref, sh_ref)

    @pl.when(i == 0)
    def _():
        _ffn_init(carry_ref)
        for _ in _rglru_tile(x0_ref, gate0_ref, True, lru, ylru_ref, 0):
            pass

    next_starts_seq = (i + 1) % tiles_per_seq == 0
    after = [None]
    side = _rglru_tile(xn_ref, gaten_ref, next_starts_seq, lru, ylru_ref, 1 - slot, after=after)
    next(side)
    half = o_ref.shape[0] // 2
    for r0 in (0, half):
        rows = slice(r0, r0 + half)
        o_ref[rows, :] = (h_ref[rows, :] + _dot(ya_ref[rows, :], woa_ref[...])
                          + _dot(ylru_ref[slot, rows, :], wob_ref[...]))
    _ffn(o_ref, next_starts_seq, g_ref, wg_ref, wu_ref, cw_ref, cb_ref, wd_ref,
         hn_ref, act_ref, carry_ref, gwork_ref, side_work=side, side_plan=LRU_SIDE_PLAN,
         side_after=after)


def _ab_out_ffn(h, ya, x_lru, gate_lru, lru_consts, woa, wob, ffn, layer, seq_len):
    n = h.shape[0]
    tm = ROW_TILE
    n_tiles = n // tm
    row = lambda c: pl.BlockSpec((tm, c), lambda i: (i, 0))
    first = pl.BlockSpec((tm, LRU_WIDTH), lambda i: (0, 0), pipeline_mode=pl.Buffered(1))
    ahead = pl.BlockSpec((tm, LRU_WIDTH), lambda i: (jnp.minimum(i + 1, n_tiles - 1), 0))
    consts = (*lru_consts, woa, wob, *ffn)
    return pl.pallas_call(
        functools.partial(_ab_out_ffn_kernel, tiles_per_seq=seq_len // tm),
        grid=(n_tiles,),
        in_specs=[row(D_MODEL), row(ya.shape[1]), first, first, ahead, ahead]
                 + [_const_spec(a.shape) for a in (*lru_consts, woa, wob)]
                 + [_layer_spec(a.shape, layer) for a in ffn],
        out_specs=row(D_MODEL),
        out_shape=jax.ShapeDtypeStruct((n, D_MODEL), F32),
        scratch_shapes=[pltpu.VMEM((2, tm, LRU_WIDTH), BF16),
                        pltpu.VMEM((LRU_WIDTH // LANES, SUBLANES + tm, LANES), F32),
                        pltpu.VMEM((SUBLANES, LRU_WIDTH), F32)]
                       + [pltpu.VMEM((LRU_WIDTH // LANES, tm, LANES), F32)] * 3 + _ffn_scratch(tm),
        compiler_params=pltpu.CompilerParams(dimension_semantics=("arbitrary",),
                                             vmem_limit_bytes=VMEM_LIMIT),
        name="ab_out_ffn",
    )(h, ya, x_lru, gate_lru, x_lru, gate_lru, *consts)


def _sgu_ffn_kernel(h_ref, cg_ref, win_ref, lng_ref, lnb_ref, ws_ref, bs_ref, wout_ref,
                    g_ref, wg_ref, wu_ref, cw_ref, cb_ref, wd_ref, fg_ref, o_ref,
                    u_ref, v_ref, gated_ref, hn_ref, act_ref, carry_ref, gwork_ref, *, tiles_per_seq):
    i = pl.program_id(0)

    @pl.when(i == 0)
    def _():
        _ffn_init(carry_ref)

    tiles = h_ref.shape[0] // ROW_TILE
    for sub in range(tiles):
        rows = pl.ds(sub * ROW_TILE, ROW_TILE)
        next_starts_seq = (i * tiles + sub + 1) % tiles_per_seq == 0
        _sgu_ffn_tile(h_ref.at[rows], cg_ref, win_ref, lng_ref, lnb_ref, ws_ref, bs_ref, wout_ref,
                      g_ref, wg_ref, wu_ref, cw_ref, cb_ref, wd_ref, fg_ref, o_ref.at[rows],
                      u_ref, v_ref, gated_ref, hn_ref, act_ref, carry_ref, gwork_ref, next_starts_seq)


def _sgu_ffn_tile(h_ref, cg_ref, win_ref, lng_ref, lnb_ref, ws_ref, bs_ref, wout_ref,
                  g_ref, wg_ref, wu_ref, cw_ref, cb_ref, wd_ref, fg_ref, o_ref,
                  u_ref, v_ref, gated_ref, hn_ref, act_ref, carry_ref, gwork_ref, next_starts_seq):
    tm = h_ref.shape[0]
    h = h_ref[...]
    xn = _rms(h, cg_ref[...]).astype(BF16)
    v = _gelu_x2(_dot(xn, win_ref[:, D_MODEL:]))
    mu = jnp.mean(v, axis=-1, keepdims=True)
    vc = v - mu
    var = jnp.mean(vc * vc, axis=-1, keepdims=True)
    v_ref[...] = (vc * lax.rsqrt(var + 4.0 * NORM_EPS) * lng_ref[...] + lnb_ref[...]).astype(BF16)
    u_ref[...] = _gelu_x2(_dot(xn, win_ref[:, :D_MODEL]))

    n_chunks = tm // CHUNK
    r = lax.broadcasted_iota(jnp.int32, (CHUNK, CHUNK), 0)
    c = lax.broadcasted_iota(jnp.int32, (CHUNK, CHUNK), 1)
    for gp in range(SGU_GROUPS):
        lanes = slice(gp * CHUNK, (gp + 1) * CHUNK)
        w = jnp.where(c <= r, ws_ref[gp], 0.0).astype(BF16)
        rhs = jnp.concatenate([v_ref[ck * CHUNK:(ck + 1) * CHUNK, lanes] for ck in range(n_chunks)],
                              axis=1)
        sg = _dot(w, rhs) + bs_ref[:, gp:gp + 1]
        for ck in range(n_chunks):
            rows = slice(ck * CHUNK, (ck + 1) * CHUNK)
            gated_ref[rows, lanes] = (u_ref[rows, lanes] * sg[:, ck * CHUNK:(ck + 1) * CHUNK]).astype(BF16)

    o_ref[...] = h_ref[...] + _dot(gated_ref[...], wout_ref[...])
    _ffn(o_ref, next_starts_seq, g_ref, wg_ref, wu_ref, cw_ref, cb_ref, wd_ref, hn_ref, act_ref,
         carry_ref, gwork_ref)
    o_ref[...] = _rms(o_ref[...], fg_ref[...])


def _sgu_ffn(h, cg, win, lng, lnb, ws, bs_t, wout, ffn, layer, fg, seq_len):
    n = h.shape[0]
    tm = ROW_TILE
    row = lambda c: pl.BlockSpec((SGU_STEP_ROWS, c), lambda i: (i, 0))
    sgu = (cg, win, lng, lnb, ws, bs_t, wout)
    consts = (*sgu, *ffn, fg)
    return pl.pallas_call(
        functools.partial(_sgu_ffn_kernel, tiles_per_seq=seq_len // tm),
        grid=(n // SGU_STEP_ROWS,),
        in_specs=[row(D_MODEL)] + [_const_spec(a.shape) for a in sgu]
                 + [_layer_spec(a.shape, layer) for a in ffn] + [_const_spec(fg.shape)],
        out_specs=row(D_MODEL),
        out_shape=jax.ShapeDtypeStruct((n, D_MODEL), F32),
        scratch_shapes=[pltpu.VMEM((tm, D_MODEL), F32), pltpu.VMEM((tm, D_MODEL), BF16),
                        pltpu.VMEM((tm, D_MODEL), BF16)] + _ffn_scratch(tm),
        compiler_params=pltpu.CompilerParams(dimension_semantics=("arbitrary",),
                                             vmem_limit_bytes=VMEM_LIMIT),
        name="sgu_ffn",
    )(h, *consts)


def _ffn_params(norm, w_gate, w_up, conv_w, conv_b, w_down):
    depth = norm.shape[0]
    return (norm.reshape(depth, 1, D_MODEL), w_gate.astype(BF16), (0.5 * w_up).astype(BF16),
            conv_w, conv_b.reshape(depth, 1, D_FF), w_down.astype(BF16))


def _block_diag(w):
    heads, blk, _ = w.shape
    eye = jnp.eye(heads, dtype=w.dtype)
    return (w[:, :, None, :] * eye[:, None, :, None]).reshape(heads * blk, heads * blk)


def kernel(x, positions, ab_norm, ab_w_in, ab_q_norm, ab_w_q_b, ab_kv_norm, ab_w_kv_b, ab_conv_w, ab_conv_b, ab_w_rg_a, ab_b_rg_a, ab_w_rg_x, ab_b_rg_x, ab_lambda, ab_w_out, c_norm, c_w_in, c_ln_g, c_ln_b, c_w_s, c_b_s, c_w_out, ffn_norm, ffn_w_gate, ffn_w_up, ffn_conv_w, ffn_conv_b, ffn_w_down, final_norm):
    b, s, d = x.shape
    n = b * s
    h = x.reshape(n, d)
    pos = positions.reshape(n, 1)

    w_in = ab_w_in[0]
    o2 = Q_LORA + KV_LORA
    o3 = o2 + QK_ROPE
    zeros = lambda c: jnp.zeros((d, c), w_in.dtype)
    w_in_p = jnp.concatenate([w_in[:, :o2], zeros(QK_NOPE), w_in[:, o2:o3],
                              zeros(HEAD_PAD - QK_NOPE - QK_ROPE), w_in[:, o3:]], axis=1).astype(BF16)
    qk = QK_NOPE + QK_ROPE
    half = QK_ROPE // 2
    wq3 = ab_w_q_b[0].reshape(Q_LORA, MLA_HEADS, qk)
    pad_q = lambda w: jnp.pad(w, ((0, 0), (0, 0), (0, HEAD_PAD - w.shape[-1]))
                              ).reshape(Q_LORA, MLA_HEADS * HEAD_PAD)
    wq_rot = jnp.concatenate([jnp.zeros_like(wq3[..., :QK_NOPE]), -wq3[..., QK_NOPE + half:],
                              wq3[..., QK_NOPE:QK_NOPE + half]], axis=-1)
    wq = jnp.concatenate([pad_q(wq3), pad_q(wq_rot)], axis=1).astype(BF16)
    wkv = ab_w_kv_b[0].reshape(KV_LORA, MLA_HEADS, QK_NOPE + V_HEAD)
    wk = jnp.pad(wkv[:, :, :QK_NOPE], ((0, 0), (0, 0), (0, HEAD_PAD - QK_NOPE))
                 ).reshape(KV_LORA, MLA_HEADS * HEAD_PAD).astype(BF16)
    wv = wkv[:, :, QK_NOPE:].reshape(KV_LORA, MLA_HEADS * V_HEAD).astype(BF16)
    freq = jnp.exp(-math.log(ROPE_BASE) * jnp.arange(half, dtype=F32) / half)
    invf = jnp.tile(freq, LANES // half).reshape(1, LANES)

    q, k, v, x_lru, gate_lru = _ab_in(
        h, pos, ab_norm[0].reshape(1, d), w_in_p, ab_q_norm[0].reshape(1, Q_LORA), wq,
        ab_kv_norm[0].reshape(1, KV_LORA), wk, wv, invf)

    y_mla = _attention(q.reshape(b, s, -1), k.reshape(b, s, -1), v.reshape(b, s, -1))
    lru_consts = (ab_conv_w[0], ab_conv_b[0].reshape(1, -1),
                  _block_diag(ab_w_rg_a[0]).astype(BF16), ab_b_rg_a[0].reshape(1, -1),
                  _block_diag(ab_w_rg_x[0]).astype(BF16), ab_b_rg_x[0].reshape(1, -1),
                  ab_lambda[0].reshape(1, -1))

    w_out = ab_w_out[0].astype(BF16)
    mla_w = MLA_HEADS * V_HEAD
    ffn = _ffn_params(ffn_norm, ffn_w_gate, ffn_w_up, ffn_conv_w, ffn_conv_b, ffn_w_down)
    h = _ab_out_ffn(h, y_mla.reshape(n, mla_w), x_lru, gate_lru, lru_consts,
                    w_out[:mla_w], 0.5 * w_out[mla_w:], ffn, 0, seq_len=s)

    out = _sgu_ffn(h, c_norm[0].reshape(1, d), c_w_in[0].astype(BF16), c_ln_g[0].reshape(1, -1),
                   c_ln_b[0].reshape(1, -1), 0.5 * c_w_s[0], 0.5 * c_b_s[0].T, c_w_out[0].astype(BF16),
                   ffn, 1, final_norm.reshape(1, d), seq_len=s)
    return out.reshape(b, s, d)
```

```python
import functools
import math

import jax
import jax.numpy as jnp
from jax import lax
from jax.experimental import pallas as pl
from jax.experimental.pallas import tpu as pltpu

F32 = jnp.float32
BF16 = jnp.bfloat16

D_MODEL = 1024
MLA_HEADS = 8
Q_LORA = 256
KV_LORA = 128
QK_NOPE = 64
QK_ROPE = 32
V_HEAD = 64
ROPE_BASE = 10000.0
LRU_WIDTH = 512
LRU_CONV = 4
LRU_C = 8.0
CHUNK = 128
SGU_GROUPS = 8
D_FF = 2816
FFN_CONV = 3
NORM_EPS = 1e-6

LANES = 128
SUBLANES = 8
MXU_TILE = 256
HEAD_PAD = 128

ROW_TILE = 512
AB_IN_TILE = 1024
AB_IN_SUB = 512
SGU_STEP_ROWS = 512
ATTN_TILE = 256
ATTN_PAIRS_PER_STEP = 1
LRU_SIDE_PLAN = (3,) * 10 + (2,) + (0,)
FF_CHUNK = 256
N_FF_CHUNKS = D_FF // FF_CHUNK
DOWN_COLS = 1024
VMEM_LIMIT = 56 * 1024 * 1024


def _dot(a, b):
    return jnp.dot(a, b, preferred_element_type=F32)


def _dot_nt(a, b):
    return lax.dot_general(a, b, (((1,), (1,)), ((), ())), preferred_element_type=F32)


def _gelu_x2(x):
    c = math.sqrt(2.0 / math.pi)
    t = jnp.tanh(x * (c + (c * 0.044715) * (x * x)))
    return x * t + x


def _sigmoid(x):
    return 0.5 * jnp.tanh(0.5 * x) + 0.5


def _expm1(x):
    u = jnp.exp(x)
    small = jnp.where(u == 1.0, x, (u - 1.0) * x / jnp.log(u))
    return jnp.where(x < -0.5, u - 1.0, small)


def _rms(x, g):
    ms = jnp.mean(x * x, axis=-1, keepdims=True)
    return x * lax.rsqrt(ms + NORM_EPS) * g


def _const_spec(shape):
    nd = len(shape)
    return pl.BlockSpec(shape, lambda *_: (0,) * nd, pipeline_mode=pl.Buffered(1))


def _layer_spec(shape, layer):
    nd = len(shape)
    return pl.BlockSpec((None,) + tuple(shape[1:]), lambda *_: (layer,) + (0,) * (nd - 1),
                        pipeline_mode=pl.Buffered(1))


def _ab_in_kernel(h_ref, pos_ref, g_ref, w_in_ref, qg_ref, wq_ref, kvg_ref, wk_ref, wv_ref,
                  invf_ref, q_ref, k_ref, v_ref, xl_ref, gate_ref, *, scale, sub):
    tm = h_ref.shape[0]
    o1 = Q_LORA
    o2 = o1 + KV_LORA
    o3 = o2 + HEAD_PAD
    o4 = o3 + LRU_WIDTH
    half = QK_ROPE // 2
    x1_lo, x2_lo, x2_hi = QK_NOPE, QK_NOPE + half, QK_NOPE + QK_ROPE

    def rope(blk, c, sd, su):
        return blk * c + pltpu.roll(blk, half, 1) * sd + pltpu.roll(blk, LANES - half, 1) * su

    def norm_stage(r):
        groups = LANES // QK_ROPE
        nb = sub // groups
        lane = lax.broadcasted_iota(jnp.int32, (nb, LANES), 1)
        pos_c = pos_ref[r.start + (groups - 1) * nb:r.start + groups * nb, :]
        for gi in range(groups - 2, -1, -1):
            pos_c = jnp.where(lane < (gi + 1) * QK_ROPE,
                              pos_ref[r.start + gi * nb:r.start + (gi + 1) * nb, :], pos_c)
        ang = pos_c.astype(F32) * invf_ref[...]
        cos_c = jnp.cos(ang)
        sin_c = jnp.sin(ang)
        c_tab, s_dn, s_up = [], [], []
        for gi in range(groups):
            shift = (x1_lo - gi * QK_ROPE) % LANES
            cosv = cos_c if shift == 0 else pltpu.roll(cos_c, shift, 1)
            sinv = sin_c if shift == 0 else pltpu.roll(sin_c, shift, 1)
            c_tab.append(jnp.where(lane < x1_lo, 1.0, jnp.where(lane < x2_hi, cosv, 0.0)))
            s_dn.append(jnp.where((lane >= x2_lo) & (lane < x2_hi), sinv, 0.0))
            s_up.append(jnp.where((lane >= x1_lo) & (lane < x2_lo), -sinv, 0.0))
        tabs = tuple(jnp.concatenate(t, axis=0) for t in (c_tab, s_dn, s_up))
        xn = _rms(h_ref[r, :], g_ref[...]).astype(BF16)
        return r, xn, tabs

    def proj_stage(r, xn, tabs):
        xl_ref[r, :] = _dot(xn, w_in_ref[:, o3:o4])
        gate_ref[r, :] = _dot(xn, w_in_ref[:, o4:])
        c_q = _dot(xn, w_in_ref[:, :o1])
        c_kv = _dot(xn, w_in_ref[:, o1:o2])
        kpe = _dot(xn, w_in_ref[:, o2:o3])
        return r, c_q, c_kv, kpe, tabs

    def latent_stage(r, c_q, c_kv, kpe, tabs):
        qn = _rms(c_q, qg_ref[...]).astype(BF16)
        width = MLA_HEADS * HEAD_PAD
        qf = _dot(qn, wq_ref[:, :width])
        qr = _dot(qn, wq_ref[:, width:])
        kvn = _rms(c_kv, kvg_ref[...]).astype(BF16)
        kf = _dot(kvn, wk_ref[...])
        v_ref[r, :] = _dot(kvn, wv_ref[...]).astype(BF16)
        return r, qf, qr, kf, kpe, tabs

    def rope_stage(r, qf, qr, kf, kpe, tabs):
        c_tab, s_dn, s_up = tabs
        cq, sq = c_tab * scale, (s_dn - s_up) * scale
        kpe_r = rope(kpe, c_tab, s_dn, s_up)
        for hd in range(MLA_HEADS):
            sl = slice(hd * HEAD_PAD, (hd + 1) * HEAD_PAD)
            q_ref[r, sl] = (qf[:, sl] * cq + qr[:, sl] * sq).astype(BF16)
            k_ref[r, sl] = (kf[:, sl] + kpe_r).astype(BF16)

    stages = (proj_stage, latent_stage, rope_stage)
    n_sub = tm // sub
    live = [None] * len(stages)
    for step in range(n_sub + len(stages)):
        nxt = [None] * len(stages)
        if step < n_sub:
            nxt[0] = norm_stage(slice(step * sub, (step + 1) * sub))
        for si, stage in enumerate(stages):
            if live[si] is not None:
                out = stage(*live[si])
                if si + 1 < len(stages):
                    nxt[si + 1] = out
        live = nxt


def _ab_in(h, pos, g, w_in, qg, wq, kvg, wk, wv, invf):
    n = h.shape[0]
    tm = AB_IN_TILE
    row = lambda c: pl.BlockSpec((tm, c), lambda i: (i, 0))
    scale = float((QK_NOPE + QK_ROPE) ** -0.5 * math.log2(math.e))
    return pl.pallas_call(
        functools.partial(_ab_in_kernel, scale=scale, sub=AB_IN_SUB),
        grid=(n // tm,),
        in_specs=[row(D_MODEL), row(1), _const_spec(g.shape), _const_spec(w_in.shape),
                  _const_spec(qg.shape), _const_spec(wq.shape), _const_spec(kvg.shape),
                  _const_spec(wk.shape), _const_spec(wv.shape), _const_spec(invf.shape)],
        out_specs=[row(MLA_HEADS * HEAD_PAD), row(MLA_HEADS * HEAD_PAD), row(MLA_HEADS * V_HEAD),
                   row(LRU_WIDTH), row(LRU_WIDTH)],
        out_shape=[jax.ShapeDtypeStruct((n, MLA_HEADS * HEAD_PAD), BF16),
                   jax.ShapeDtypeStruct((n, MLA_HEADS * HEAD_PAD), BF16),
                   jax.ShapeDtypeStruct((n, MLA_HEADS * V_HEAD), BF16),
                   jax.ShapeDtypeStruct((n, LRU_WIDTH), F32),
                   jax.ShapeDtypeStruct((n, LRU_WIDTH), F32)],
        compiler_params=pltpu.CompilerParams(dimension_semantics=("arbitrary",),
                                             vmem_limit_bytes=VMEM_LIMIT),
        name="ab_in",
    )(h, pos, g, w_in, qg, wq, kvg, wk, wv, invf)


def _attn_kernel(q_ref, k_ref, v_ref, o_ref):
    s_len = q_ref.shape[0]
    t = ATTN_TILE
    r = lax.broadcasted_iota(jnp.int32, (t, t), 0)
    c = lax.broadcasted_iota(jnp.int32, (t, t), 1)
    causal = c <= r
    lane = lax.broadcasted_iota(jnp.int32, (t, LANES), 1)
    n_tiles = s_len // t

    def scores(i, hd):
        hl = slice(hd * HEAD_PAD, (hd + 1) * HEAD_PAD)
        return _dot_nt(q_ref[i * t:(i + 1) * t, hl], k_ref[:(i + 1) * t, hl])

    def softmax(i, s):
        kv = (i + 1) * t
        diag = jnp.where(causal, s[:, kv - t:], -jnp.inf)
        s = diag if i == 0 else jnp.concatenate([s[:, :kv - t], diag], axis=1)
        p = jnp.exp2(s - jnp.max(s, axis=1, keepdims=True))
        return p.astype(BF16), jnp.sum(p, axis=1, keepdims=True)

    def values(i, pair, p, l):
        return _dot(p, v_ref[:(i + 1) * t, pair * LANES:(pair + 1) * LANES]) / l

    n_pairs = q_ref.shape[1] // (2 * HEAD_PAD)
    order = [(pair, i) for pair in range(n_pairs) for i in range(n_tiles - 1, -1, -1)]
    s_cur = p_cur = None
    for step in range(len(order) + 2):
        s_next = p_next = None
        if step < len(order):
            pair, i = order[step]
            s_next = ([scores(i, 2 * pair + hd) for hd in range(2)], pair, i)
        if s_cur is not None:
            p_next = ([softmax(s_cur[2], s) for s in s_cur[0]], s_cur[1], s_cur[2])
        if p_cur is not None:
            _, pair, i = p_cur
            o0, o1 = (values(i, pair, p, l) for p, l in p_cur[0])
            o_ref[i * t:(i + 1) * t, pair * LANES:(pair + 1) * LANES] = jnp.where(
                lane < V_HEAD, o0, o1).astype(o_ref.dtype)
        s_cur, p_cur = s_next, p_next


def _attention(q, k, v):
    b, s, _ = q.shape
    pp = ATTN_PAIRS_PER_STEP
    steps = MLA_HEADS // (2 * pp)
    return pl.pallas_call(
        _attn_kernel,
        grid=(b, steps),
        in_specs=[pl.BlockSpec((None, s, pp * 2 * HEAD_PAD), lambda bi, p: (bi, 0, p)),
                  pl.BlockSpec((None, s, pp * 2 * HEAD_PAD), lambda bi, p: (bi, 0, p)),
                  pl.BlockSpec((None, s, pp * 2 * V_HEAD), lambda bi, p: (bi, 0, p))],
        out_specs=pl.BlockSpec((None, s, pp * 2 * V_HEAD), lambda bi, p: (bi, 0, p)),
        out_shape=jax.ShapeDtypeStruct((b, s, MLA_HEADS * V_HEAD), BF16),
        compiler_params=pltpu.CompilerParams(dimension_semantics=("arbitrary", "arbitrary"),
                                             vmem_limit_bytes=VMEM_LIMIT),
        name="mla_attn",
    )(q, k, v)


def _rglru_tile(x_ref, gate_ref, first, lru, y_ref, slot):
    (cw_ref, cb_ref, wa_ref, ba_ref, wx_ref, bx_ref, lam_ref,
     xprev_ref, hprev_ref, a_ref, b_ref, h_ref) = lru
    ts, c = x_ref.shape
    slabs = c // LANES
    if first is True:
        h_in = [jnp.zeros((SUBLANES, LANES), F32)] * slabs
    else:
        h_in = [jnp.where(first, 0.0, hprev_ref[:, j * LANES:(j + 1) * LANES]) for j in range(slabs)]
    xcs = []
    for j in range(slabs):
        lanes = slice(j * LANES, (j + 1) * LANES)
        xj = x_ref[:, lanes]
        if first is True:
            xprev_ref[j, :SUBLANES, :] = jnp.zeros((SUBLANES, LANES), F32)
        else:
            xprev_ref[j, :SUBLANES, :] = jnp.where(first, 0.0, xprev_ref[j, :SUBLANES, :])
        xprev_ref[j, SUBLANES:, :] = xj
        xcj = cb_ref[:, lanes] + cw_ref[LRU_CONV - 1:LRU_CONV, lanes] * xj
        for k in range(1, LRU_CONV):
            tap = xprev_ref[j, pl.ds(SUBLANES - k, ts, stride=1), :]
            xcj = xcj + cw_ref[LRU_CONV - 1 - k:LRU_CONV - k, lanes] * tap
        xprev_ref[j, :SUBLANES, :] = xj[ts - SUBLANES:]
        xcs.append(xcj)
    xc = jnp.concatenate(xcs, axis=1)
    xb = xc.astype(BF16)

    def gate_pre(w_ref, b_ref):
        parts = [_dot(xb[:, t0:t0 + MXU_TILE], w_ref[t0:t0 + MXU_TILE, t0:t0 + MXU_TILE])
                 for t0 in range(0, c, MXU_TILE)]
        return jnp.concatenate(parts, axis=1) + b_ref[...]

    pre_a = gate_pre(wa_ref, ba_ref)
    pre_x = gate_pre(wx_ref, bx_ref)
    z = -lam_ref[...]
    softplus = jnp.maximum(z, 0.0) + jnp.log1p(jnp.exp(-jnp.abs(z)))
    row = lax.broadcasted_iota(jnp.int32, (SUBLANES, LANES), 0)
    yield

    blk = SUBLANES * SUBLANES
    for r0 in range(0, ts, blk):
        rows = slice(r0, r0 + blk)
        for j in range(slabs):
            lanes = slice(j * LANES, (j + 1) * LANES)
            log_a = (-LRU_C) * _sigmoid(pre_a[rows, lanes]) * softplus[:, lanes]
            a_ref[j, rows, :] = jnp.exp(log_a)
            b_ref[j, rows, :] = (jnp.sqrt(-_expm1(2.0 * log_a))
                                 * (_sigmoid(pre_x[rows, lanes]) * xc[rows, lanes]))
            hs, ps = [], []
            for r in range(SUBLANES):
                a = a_ref[j, pl.ds(r0 + r, SUBLANES, stride=SUBLANES), :]
                b = b_ref[j, pl.ds(r0 + r, SUBLANES, stride=SUBLANES), :]
                hs.append(b if r == 0 else a * hs[-1] + b)
                ps.append(a if r == 0 else a * ps[-1])
            p, q = ps[-1], hs[-1]
            for k in (1, 2, 4):
                p_sh = jnp.where(row >= k, pltpu.roll(p, k, 0), 1.0)
                q_sh = jnp.where(row >= k, pltpu.roll(q, k, 0), 0.0)
                q = p * q_sh + q
                p = p * p_sh
            seg_out = p * h_in[j] + q
            seg_in = jnp.where(row == 0, h_in[j], pltpu.roll(seg_out, 1, 0))
            for r in range(SUBLANES):
                h_ref[j, pl.ds(r0 + r, SUBLANES, stride=SUBLANES), :] = ps[r] * seg_in + hs[r]
            h_in[j] = jnp.broadcast_to(seg_out[SUBLANES - 1:, :], (SUBLANES, LANES))
            y = h_ref[j, rows, :] * _gelu_x2(gate_ref[rows, lanes])
            y_ref[slot, rows, lanes] = y.astype(y_ref.dtype)
            yield
    hprev_ref[...] = jnp.concatenate(h_in, axis=1)


def _ffn_init(carry_ref):
    carry_ref[...] = jnp.zeros_like(carry_ref)


def _ffn(res_ref, next_starts_seq, g_ref, wg_ref, wu_ref, cw_ref, cb_ref, wd_ref, hn_ref, act_ref,
         carry_ref, gwork_ref, side_work=None, side_plan=None, starts_seq=None):
    tm = res_ref.shape[0]
    for r0 in range(0, tm, tm // 2):
        rows = slice(r0, r0 + tm // 2)
        hn_ref[rows, :] = _rms(res_ref[rows, :], g_ref[...]).astype(BF16)

    if starts_seq is not None:
        @pl.when(starts_seq)
        def _():
            _ffn_init(carry_ref)

    for f in range(N_FF_CHUNKS):
        cols = slice(f * FF_CHUNK, (f + 1) * FF_CHUNK)
        hn = hn_ref[...]
        g = _dot(hn, wg_ref[:, cols])
        u = _dot(hn, wu_ref[:, cols])
        for j in range(FF_CHUNK // LANES):
            slab = f * (FF_CHUNK // LANES) + j
            work = slab % gwork_ref.shape[0]
            lanes = slice(cols.start + j * LANES, cols.start + (j + 1) * LANES)
            gj = g[:, j * LANES:(j + 1) * LANES]
            gwork_ref[work, :SUBLANES, :] = carry_ref[slab]
            gwork_ref[work, SUBLANES:, :] = gj
            for h0 in range(0, tm, tm // 2):
                hrows = slice(h0, h0 + tm // 2)
                yj = cb_ref[:, lanes] + cw_ref[FFN_CONV - 1:FFN_CONV, lanes] * gj[hrows]
                for k in range(1, FFN_CONV):
                    tap = gwork_ref[work, pl.ds(SUBLANES - k + h0, tm // 2, stride=1), :]
                    yj = yj + cw_ref[FFN_CONV - 1 - k:FFN_CONV - k, lanes] * tap
                uj = u[hrows, j * LANES:(j + 1) * LANES]
                act_ref[hrows, lanes] = (_gelu_x2(yj) * uj).astype(BF16)
            carry_ref[slab] = jnp.where(next_starts_seq, 0.0, gj[tm - SUBLANES:])
        if side_work is not None:
            for _ in range(side_plan[f]):
                next(side_work, None)
    for c0 in range(0, D_MODEL, DOWN_COLS):
        cols = slice(c0, c0 + DOWN_COLS)
        res_ref[:, cols] = res_ref[:, cols] + _dot(act_ref[...], wd_ref[:, cols])
        if side_work is not None:
            for _ in range(side_plan[N_FF_CHUNKS + c0 // DOWN_COLS]):
                next(side_work, None)
    if side_work is not None:
        for _ in side_work:
            pass


def _ffn_scratch(tm):
    return [pltpu.VMEM((tm, D_MODEL), BF16), pltpu.VMEM((tm, D_FF), BF16),
            pltpu.VMEM((D_FF // LANES, SUBLANES, LANES), F32),
            pltpu.VMEM((2 * FF_CHUNK // LANES, SUBLANES + tm, LANES), F32)]


def _ab_out_ffn_kernel(h_ref, ya_ref, x0_ref, gate0_ref, xn_ref, gaten_ref,
                       lcw_ref, lcb_ref, wa_ref, ba_ref, wx_ref, bx_ref, lam_ref,
                       woa_ref, wob_ref, g_ref, wg_ref, wu_ref, cw_ref, cb_ref, wd_ref, o_ref,
                       ylru_ref, xprev_ref, hprev_ref, sa_ref, sb_ref, sh_ref,
                       hn_ref, act_ref, carry_ref, gwork_ref, *, tiles_per_seq):
    i = pl.program_id(0)
    slot = i % 2
    lru = (lcw_ref, lcb_ref, wa_ref, ba_ref, wx_ref, bx_ref, lam_ref, xprev_ref, hprev_ref,
           sa_ref, sb_ref, sh_ref)

    @pl.when(i == 0)
    def _():
        _ffn_init(carry_ref)
        for _ in _rglru_tile(x0_ref, gate0_ref, True, lru, ylru_ref, 0):
            pass

    next_starts_seq = (i + 1) % tiles_per_seq == 0
    side = _rglru_tile(xn_ref, gaten_ref, next_starts_seq, lru, ylru_ref, 1 - slot)
    next(side)
    half = o_ref.shape[0] // 2
    for r0 in (0, half):
        rows = slice(r0, r0 + half)
        o_ref[rows, :] = (h_ref[rows, :] + _dot(ya_ref[rows, :], woa_ref[...])
                          + _dot(ylru_ref[slot, rows, :], wob_ref[...]))
    _ffn(o_ref, next_starts_seq, g_ref, wg_ref, wu_ref, cw_ref, cb_ref, wd_ref,
         hn_ref, act_ref, carry_ref, gwork_ref, side_work=side, side_plan=LRU_SIDE_PLAN,
         starts_seq=i % tiles_per_seq == 0)


def _ab_out_ffn(h, ya, x_lru, gate_lru, lru_consts, woa, wob, ffn, layer, seq_len):
    n = h.shape[0]
    tm = ROW_TILE
    n_tiles = n // tm
    row = lambda c: pl.BlockSpec((tm, c), lambda i: (i, 0))
    first = pl.BlockSpec((tm, LRU_WIDTH), lambda i: (0, 0), pipeline_mode=pl.Buffered(1))
    ahead = pl.BlockSpec((tm, LRU_WIDTH), lambda i: (jnp.minimum(i + 1, n_tiles - 1), 0))
    consts = (*lru_consts, woa, wob, *ffn)
    return pl.pallas_call(
        functools.partial(_ab_out_ffn_kernel, tiles_per_seq=seq_len // tm),
        grid=(n_tiles,),
        in_specs=[row(D_MODEL), row(ya.shape[1]), first, first, ahead, ahead]
                 + [_const_spec(a.shape) for a in (*lru_consts, woa, wob)]
                 + [_layer_spec(a.shape, layer) for a in ffn],
        out_specs=row(D_MODEL),
        out_shape=jax.ShapeDtypeStruct((n, D_MODEL), F32),
        scratch_shapes=[pltpu.VMEM((2, tm, LRU_WIDTH), BF16),
                        pltpu.VMEM((LRU_WIDTH // LANES, SUBLANES + tm, LANES), F32),
                        pltpu.VMEM((SUBLANES, LRU_WIDTH), F32)]
                       + [pltpu.VMEM((LRU_WIDTH // LANES, tm, LANES), F32)] * 3 + _ffn_scratch(tm),
        compiler_params=pltpu.CompilerParams(dimension_semantics=("arbitrary",),
                                             vmem_limit_bytes=VMEM_LIMIT),
        name="ab_out_ffn",
    )(h, ya, x_lru, gate_lru, x_lru, gate_lru, *consts)


def _sgu_ffn_kernel(h_ref, cg_ref, win_ref, lng_ref, lnb_ref, ws_ref, bs_ref, wout_ref,
                    g_ref, wg_ref, wu_ref, cw_ref, cb_ref, wd_ref, fg_ref, o_ref,
                    u_ref, v_ref, gated_ref, hn_ref, act_ref, carry_ref, gwork_ref, *, tiles_per_seq):
    i = pl.program_id(0)

    @pl.when(i == 0)
    def _():
        _ffn_init(carry_ref)

    tiles = h_ref.shape[0] // ROW_TILE
    for sub in range(tiles):
        rows = pl.ds(sub * ROW_TILE, ROW_TILE)
        next_starts_seq = (i * tiles + sub + 1) % tiles_per_seq == 0
        _sgu_ffn_tile(h_ref.at[rows], cg_ref, win_ref, lng_ref, lnb_ref, ws_ref, bs_ref, wout_ref,
                      g_ref, wg_ref, wu_ref, cw_ref, cb_ref, wd_ref, fg_ref, o_ref.at[rows],
                      u_ref, v_ref, gated_ref, hn_ref, act_ref, carry_ref, gwork_ref, next_starts_seq)


def _sgu_ffn_tile(h_ref, cg_ref, win_ref, lng_ref, lnb_ref, ws_ref, bs_ref, wout_ref,
                  g_ref, wg_ref, wu_ref, cw_ref, cb_ref, wd_ref, fg_ref, o_ref,
                  u_ref, v_ref, gated_ref, hn_ref, act_ref, carry_ref, gwork_ref, next_starts_seq):
    tm = h_ref.shape[0]
    h = h_ref[...]
    xn = _rms(h, cg_ref[...]).astype(BF16)
    v = _gelu_x2(_dot(xn, win_ref[:, D_MODEL:]))
    mu = jnp.mean(v, axis=-1, keepdims=True)
    vc = v - mu
    var = jnp.mean(vc * vc, axis=-1, keepdims=True)
    v_ref[...] = (vc * lax.rsqrt(var + 4.0 * NORM_EPS) * lng_ref[...] + lnb_ref[...]).astype(BF16)
    u_ref[...] = _gelu_x2(_dot(xn, win_ref[:, :D_MODEL]))

    n_chunks = tm // CHUNK
    r = lax.broadcasted_iota(jnp.int32, (CHUNK, CHUNK), 0)
    c = lax.broadcasted_iota(jnp.int32, (CHUNK, CHUNK), 1)
    for gp in range(SGU_GROUPS):
        lanes = slice(gp * CHUNK, (gp + 1) * CHUNK)
        w = jnp.where(c <= r, ws_ref[gp], 0.0).astype(BF16)
        rhs = jnp.concatenate([v_ref[ck * CHUNK:(ck + 1) * CHUNK, lanes] for ck in range(n_chunks)],
                              axis=1)
        sg = _dot(w, rhs) + bs_ref[:, gp:gp + 1]
        for ck in range(n_chunks):
            rows = slice(ck * CHUNK, (ck + 1) * CHUNK)
            gated_ref[rows, lanes] = (u_ref[rows, lanes] * sg[:, ck * CHUNK:(ck + 1) * CHUNK]).astype(BF16)

    o_ref[...] = h_ref[...] + _dot(gated_ref[...], wout_ref[...])
    _ffn(o_ref, next_starts_seq, g_ref, wg_ref, wu_ref, cw_ref, cb_ref, wd_ref, hn_ref, act_ref,
         carry_ref, gwork_ref)
    o_ref[...] = _rms(o_ref[...], fg_ref[...])


def _sgu_ffn(h, cg, win, lng, lnb, ws, bs_t, wout, ffn, layer, fg, seq_len):
    n = h.shape[0]
    tm = ROW_TILE
    row = lambda c: pl.BlockSpec((SGU_STEP_ROWS, c), lambda i: (i, 0))
    sgu = (cg, win, lng, lnb, ws, bs_t, wout)
    consts = (*sgu, *ffn, fg)
    return pl.pallas_call(
        functools.partial(_sgu_ffn_kernel, tiles_per_seq=seq_len // tm),
        grid=(n // SGU_STEP_ROWS,),
        in_specs=[row(D_MODEL)] + [_const_spec(a.shape) for a in sgu]
                 + [_layer_spec(a.shape, layer) for a in ffn] + [_const_spec(fg.shape)],
        out_specs=row(D_MODEL),
        out_shape=jax.ShapeDtypeStruct((n, D_MODEL), F32),
        scratch_shapes=[pltpu.VMEM((tm, D_MODEL), F32), pltpu.VMEM((tm, D_MODEL), BF16),
                        pltpu.VMEM((tm, D_MODEL), BF16)] + _ffn_scratch(tm),
        compiler_params=pltpu.CompilerParams(dimension_semantics=("arbitrary",),
                                             vmem_limit_bytes=VMEM_LIMIT),
        name="sgu_ffn",
    )(h, *consts)


def _ffn_params(norm, w_gate, w_up, conv_w, conv_b, w_down):
    depth = norm.shape[0]
    return (norm.reshape(depth, 1, D_MODEL), w_gate.astype(BF16), (0.5 * w_up).astype(BF16),
            conv_w, conv_b.reshape(depth, 1, D_FF), w_down.astype(BF16))


def _block_diag(w):
    heads, blk, _ = w.shape
    eye = jnp.eye(heads, dtype=w.dtype)
    return (w[:, :, None, :] * eye[:, None, :, None]).reshape(heads * blk, heads * blk)


def kernel(x, positions, ab_norm, ab_w_in, ab_q_norm, ab_w_q_b, ab_kv_norm, ab_w_kv_b, ab_conv_w, ab_conv_b, ab_w_rg_a, ab_b_rg_a, ab_w_rg_x, ab_b_rg_x, ab_lambda, ab_w_out, c_norm, c_w_in, c_ln_g, c_ln_b, c_w_s, c_b_s, c_w_out, ffn_norm, ffn_w_gate, ffn_w_up, ffn_conv_w, ffn_conv_b, ffn_w_down, final_norm):
    b, s, d = x.shape
    n = b * s
    h = x.reshape(n, d)
    pos = positions.reshape(n, 1)

    w_in = ab_w_in[0]
    o2 = Q_LORA + KV_LORA
    o3 = o2 + QK_ROPE
    zeros = lambda c: jnp.zeros((d, c), w_in.dtype)
    w_in_p = jnp.concatenate([w_in[:, :o2], zeros(QK_NOPE), w_in[:, o2:o3],
                              zeros(HEAD_PAD - QK_NOPE - QK_ROPE), w_in[:, o3:]], axis=1).astype(BF16)
    qk = QK_NOPE + QK_ROPE
    half = QK_ROPE // 2
    wq3 = ab_w_q_b[0].reshape(Q_LORA, MLA_HEADS, qk)
    pad_q = lambda w: jnp.pad(w, ((0, 0), (0, 0), (0, HEAD_PAD - w.shape[-1]))
                              ).reshape(Q_LORA, MLA_HEADS * HEAD_PAD)
    wq_rot = jnp.concatenate([jnp.zeros_like(wq3[..., :QK_NOPE]), -wq3[..., QK_NOPE + half:],
                              wq3[..., QK_NOPE:QK_NOPE + half]], axis=-1)
    wq = jnp.concatenate([pad_q(wq3), pad_q(wq_rot)], axis=1).astype(BF16)
    wkv = ab_w_kv_b[0].reshape(KV_LORA, MLA_HEADS, QK_NOPE + V_HEAD)
    wk = jnp.pad(wkv[:, :, :QK_NOPE], ((0, 0), (0, 0), (0, HEAD_PAD - QK_NOPE))
                 ).reshape(KV_LORA, MLA_HEADS * HEAD_PAD).astype(BF16)
    wv = wkv[:, :, QK_NOPE:].reshape(KV_LORA, MLA_HEADS * V_HEAD).astype(BF16)
    freq = jnp.exp(-math.log(ROPE_BASE) * jnp.arange(half, dtype=F32) / half)
    invf = jnp.tile(freq, LANES // half).reshape(1, LANES)

    q, k, v, x_lru, gate_lru = _ab_in(
        h, pos, ab_norm[0].reshape(1, d), w_in_p, ab_q_norm[0].reshape(1, Q_LORA), wq,
        ab_kv_norm[0].reshape(1, KV_LORA), wk, wv, invf)

    y_mla = _attention(q.reshape(b, s, -1), k.reshape(b, s, -1), v.reshape(b, s, -1))
    lru_consts = (ab_conv_w[0], ab_conv_b[0].reshape(1, -1),
                  _block_diag(ab_w_rg_a[0]).astype(BF16), ab_b_rg_a[0].reshape(1, -1),
                  _block_diag(ab_w_rg_x[0]).astype(BF16), ab_b_rg_x[0].reshape(1, -1),
                  ab_lambda[0].reshape(1, -1))

    w_out = ab_w_out[0].astype(BF16)
    mla_w = MLA_HEADS * V_HEAD
    ffn = _ffn_params(ffn_norm, ffn_w_gate, ffn_w_up, ffn_conv_w, ffn_conv_b, ffn_w_down)
    h = _ab_out_ffn(h, y_mla.reshape(n, mla_w), x_lru, gate_lru, lru_consts,
                    w_out[:mla_w], 0.5 * w_out[mla_w:], ffn, 0, seq_len=s)

    out = _sgu_ffn(h, c_norm[0].reshape(1, d), c_w_in[0].astype(BF16), c_ln_g[0].reshape(1, -1),
                   c_ln_b[0].reshape(1, -1), 0.5 * c_w_s[0], 0.5 * c_b_s[0].T, c_w_out[0].astype(BF16),
                   ffn, 1, final_norm.reshape(1, d), seq_len=s)
    return out.reshape(b, s, d)
```

```python
import functools
import math

import jax
import jax.numpy as jnp
from jax import lax
from jax.experimental import pallas as pl
from jax.experimental.pallas import tpu as pltpu

F32 = jnp.float32
BF16 = jnp.bfloat16

D_MODEL = 1024
MLA_HEADS = 8
Q_LORA = 256
KV_LORA = 128
QK_NOPE = 64
QK_ROPE = 32
V_HEAD = 64
ROPE_BASE = 10000.0
LRU_WIDTH = 512
LRU_CONV = 4
LRU_C = 8.0
CHUNK = 128
SGU_GROUPS = 8
D_FF = 2816
FFN_CONV = 3
NORM_EPS = 1e-6

LANES = 128
SUBLANES = 8
MXU_TILE = 256
HEAD_PAD = 128

ROW_TILE = 512
AB_IN_TILE = 1024
AB_IN_SUB = 512
SGU_STEP_ROWS = 512
ATTN_TILE = 256
ATTN_PAIRS_PER_STEP = 1
LRU_SIDE_PLAN = (3,) * 10 + (2,) + (0,)
FF_CHUNK = 256
N_FF_CHUNKS = D_FF // FF_CHUNK
DOWN_COLS = 1024
VMEM_LIMIT = 56 * 1024 * 1024


def _dot(a, b):
    return jnp.dot(a, b, preferred_element_type=F32)


def _dot_nt(a, b):
    return lax.dot_general(a, b, (((1,), (1,)), ((), ())), preferred_element_type=F32)


def _gelu_x2(x):
    c = math.sqrt(2.0 / math.pi)
    t = jnp.tanh(x * (c + (c * 0.044715) * (x * x)))
    return x * t + x


def _sigmoid(x):
    return 0.5 * jnp.tanh(0.5 * x) + 0.5


def _expm1(x):
    u = jnp.exp(x)
    small = jnp.where(u == 1.0, x, (u - 1.0) * x / jnp.log(u))
    return jnp.where(x < -0.5, u - 1.0, small)


def _rms(x, g):
    ms = jnp.mean(x * x, axis=-1, keepdims=True)
    return x * lax.rsqrt(ms + NORM_EPS) * g


def _const_spec(shape):
    nd = len(shape)
    return pl.BlockSpec(shape, lambda *_: (0,) * nd, pipeline_mode=pl.Buffered(1))


def _layer_spec(shape, layer):
    nd = len(shape)
    return pl.BlockSpec((None,) + tuple(shape[1:]), lambda *_: (layer,) + (0,) * (nd - 1),
                        pipeline_mode=pl.Buffered(1))


def _ab_in_kernel(h_ref, pos_ref, g_ref, w_in_ref, qg_ref, wq_ref, kvg_ref, wk_ref, wv_ref,
                  invf_ref, q_ref, k_ref, v_ref, xl_ref, gate_ref, *, scale, sub):
    tm = h_ref.shape[0]
    o1 = Q_LORA
    o2 = o1 + KV_LORA
    o3 = o2 + HEAD_PAD
    o4 = o3 + LRU_WIDTH
    half = QK_ROPE // 2
    x1_lo, x2_lo, x2_hi = QK_NOPE, QK_NOPE + half, QK_NOPE + QK_ROPE

    def rope(blk, c, sd, su):
        return blk * c + pltpu.roll(blk, half, 1) * sd + pltpu.roll(blk, LANES - half, 1) * su

    def norm_stage(r):
        groups = LANES // QK_ROPE
        nb = sub // groups
        lane = lax.broadcasted_iota(jnp.int32, (nb, LANES), 1)
        pos_c = pos_ref[r.start + (groups - 1) * nb:r.start + groups * nb, :]
        for gi in range(groups - 2, -1, -1):
            pos_c = jnp.where(lane < (gi + 1) * QK_ROPE,
                              pos_ref[r.start + gi * nb:r.start + (gi + 1) * nb, :], pos_c)
        ang = pos_c.astype(F32) * invf_ref[...]
        cos_c = jnp.cos(ang)
        sin_c = jnp.sin(ang)
        c_tab, s_dn, s_up = [], [], []
        for gi in range(groups):
            shift = (x1_lo - gi * QK_ROPE) % LANES
            cosv = cos_c if shift == 0 else pltpu.roll(cos_c, shift, 1)
            sinv = sin_c if shift == 0 else pltpu.roll(sin_c, shift, 1)
            c_tab.append(jnp.where(lane < x1_lo, 1.0, jnp.where(lane < x2_hi, cosv, 0.0)))
            s_dn.append(jnp.where((lane >= x2_lo) & (lane < x2_hi), sinv, 0.0))
            s_up.append(jnp.where((lane >= x1_lo) & (lane < x2_lo), -sinv, 0.0))
        tabs = tuple(jnp.concatenate(t, axis=0) for t in (c_tab, s_dn, s_up))
        xn = _rms(h_ref[r, :], g_ref[...]).astype(BF16)
        return r, xn, tabs

    def proj_stage(r, xn, tabs):
        xl_ref[r, :] = _dot(xn, w_in_ref[:, o3:o4])
        gate_ref[r, :] = _dot(xn, w_in_ref[:, o4:])
        c_q = _dot(xn, w_in_ref[:, :o1])
        c_kv = _dot(xn, w_in_ref[:, o1:o2])
        kpe = _dot(xn, w_in_ref[:, o2:o3])
        return r, c_q, c_kv, kpe, tabs

    def latent_stage(r, c_q, c_kv, kpe, tabs):
        qn = _rms(c_q, qg_ref[...]).astype(BF16)
        width = MLA_HEADS * HEAD_PAD
        qf = _dot(qn, wq_ref[:, :width])
        qr = _dot(qn, wq_ref[:, width:])
        kvn = _rms(c_kv, kvg_ref[...]).astype(BF16)
        kf = _dot(kvn, wk_ref[...])
        v_ref[r, :] = _dot(kvn, wv_ref[...]).astype(BF16)
        return r, qf, qr, kf, kpe, tabs

    def rope_stage(r, qf, qr, kf, kpe, tabs):
        c_tab, s_dn, s_up = tabs
        cq, sq = c_tab * scale, (s_dn - s_up) * scale
        kpe_r = rope(kpe, c_tab, s_dn, s_up)
        for hd in range(MLA_HEADS):
            sl = slice(hd * HEAD_PAD, (hd + 1) * HEAD_PAD)
            q_ref[r, sl] = (qf[:, sl] * cq + qr[:, sl] * sq).astype(BF16)
            k_ref[r, sl] = (kf[:, sl] + kpe_r).astype(BF16)

    stages = (proj_stage, latent_stage, rope_stage)
    n_sub = tm // sub
    live = [None] * len(stages)
    for step in range(n_sub + len(stages)):
        nxt = [None] * len(stages)
        if step < n_sub:
            nxt[0] = norm_stage(slice(step * sub, (step + 1) * sub))
        for si, stage in enumerate(stages):
            if live[si] is not None:
                out = stage(*live[si])
                if si + 1 < len(stages):
                    nxt[si + 1] = out
        live = nxt


def _ab_in(h, pos, g, w_in, qg, wq, kvg, wk, wv, invf):
    n = h.shape[0]
    tm = AB_IN_TILE
    row = lambda c: pl.BlockSpec((tm, c), lambda i: (i, 0))
    scale = float((QK_NOPE + QK_ROPE) ** -0.5 * math.log2(math.e))
    return pl.pallas_call(
        functools.partial(_ab_in_kernel, scale=scale, sub=AB_IN_SUB),
        grid=(n // tm,),
        in_specs=[row(D_MODEL), row(1), _const_spec(g.shape), _const_spec(w_in.shape),
                  _const_spec(qg.shape), _const_spec(wq.shape), _const_spec(kvg.shape),
                  _const_spec(wk.shape), _const_spec(wv.shape), _const_spec(invf.shape)],
        out_specs=[row(MLA_HEADS * HEAD_PAD), row(MLA_HEADS * HEAD_PAD), row(MLA_HEADS * V_HEAD),
                   row(LRU_WIDTH), row(LRU_WIDTH)],
        out_shape=[jax.ShapeDtypeStruct((n, MLA_HEADS * HEAD_PAD), BF16),
                   jax.ShapeDtypeStruct((n, MLA_HEADS * HEAD_PAD), BF16),
                   jax.ShapeDtypeStruct((n, MLA_HEADS * V_HEAD), BF16),
                   jax.ShapeDtypeStruct((n, LRU_WIDTH), F32),
                   jax.ShapeDtypeStruct((n, LRU_WIDTH), F32)],
        compiler_params=pltpu.CompilerParams(dimension_semantics=("arbitrary",),
                                             vmem_limit_bytes=VMEM_LIMIT),
        name="ab_in",
    )(h, pos, g, w_in, qg, wq, kvg, wk, wv, invf)


def _attn_kernel(q_ref, k_ref, v_ref, o_ref):
    s_len = q_ref.shape[0]
    t = ATTN_TILE
    r = lax.broadcasted_iota(jnp.int32, (t, t), 0)
    c = lax.broadcasted_iota(jnp.int32, (t, t), 1)
    causal = c <= r
    lane = lax.broadcasted_iota(jnp.int32, (t, LANES), 1)
    n_tiles = s_len // t

    def scores(i, hd):
        hl = slice(hd * HEAD_PAD, (hd + 1) * HEAD_PAD)
        return _dot_nt(q_ref[i * t:(i + 1) * t, hl], k_ref[:(i + 1) * t, hl])

    def softmax(i, s):
        kv = (i + 1) * t
        diag = jnp.where(causal, s[:, kv - t:], -jnp.inf)
        s = diag if i == 0 else jnp.concatenate([s[:, :kv - t], diag], axis=1)
        p = jnp.exp2(s - jnp.max(s, axis=1, keepdims=True))
        return p.astype(BF16), jnp.sum(p, axis=1, keepdims=True)

    def values(i, pair, p, l):
        return _dot(p, v_ref[:(i + 1) * t, pair * LANES:(pair + 1) * LANES]) / l

    n_pairs = q_ref.shape[1] // (2 * HEAD_PAD)
    order = [(pair, i) for pair in range(n_pairs) for i in range(n_tiles - 1, -1, -1)]
    s_cur = p_cur = None
    for step in range(len(order) + 2):
        s_next = p_next = None
        if step < len(order):
            pair, i = order[step]
            s_next = ([scores(i, 2 * pair + hd) for hd in range(2)], pair, i)
        if s_cur is not None:
            p_next = ([softmax(s_cur[2], s) for s in s_cur[0]], s_cur[1], s_cur[2])
        if p_cur is not None:
            _, pair, i = p_cur
            o0, o1 = (values(i, pair, p, l) for p, l in p_cur[0])
            o_ref[i * t:(i + 1) * t, pair * LANES:(pair + 1) * LANES] = jnp.where(
                lane < V_HEAD, o0, o1).astype(o_ref.dtype)
        s_cur, p_cur = s_next, p_next


def _attention(q, k, v):
    b, s, _ = q.shape
    pp = ATTN_PAIRS_PER_STEP
    steps = MLA_HEADS // (2 * pp)
    return pl.pallas_call(
        _attn_kernel,
        grid=(b, steps),
        in_specs=[pl.BlockSpec((None, s, pp * 2 * HEAD_PAD), lambda bi, p: (bi, 0, p)),
                  pl.BlockSpec((None, s, pp * 2 * HEAD_PAD), lambda bi, p: (bi, 0, p)),
                  pl.BlockSpec((None, s, pp * 2 * V_HEAD), lambda bi, p: (bi, 0, p))],
        out_specs=pl.BlockSpec((None, s, pp * 2 * V_HEAD), lambda bi, p: (bi, 0, p)),
        out_shape=jax.ShapeDtypeStruct((b, s, MLA_HEADS * V_HEAD), BF16),
        compiler_params=pltpu.CompilerParams(dimension_semantics=("arbitrary", "arbitrary"),
                                             vmem_limit_bytes=VMEM_LIMIT),
        name="mla_attn",
    )(q, k, v)


def _rglru_tile(x_ref, gate_ref, first, lru, y_ref, slot):
    (cw_ref, cb_ref, wa_ref, ba_ref, wx_ref, bx_ref, lam_ref,
     xprev_ref, hprev_ref, a_ref, b_ref, h_ref) = lru
    ts, c = x_ref.shape
    slabs = c // LANES
    if first is True:
        h_in = [jnp.zeros((SUBLANES, LANES), F32)] * slabs
    else:
        h_in = [jnp.where(first, 0.0, hprev_ref[:, j * LANES:(j + 1) * LANES]) for j in range(slabs)]
    xcs = []
    for j in range(slabs):
        lanes = slice(j * LANES, (j + 1) * LANES)
        xj = x_ref[:, lanes]
        if first is True:
            xprev_ref[j, :SUBLANES, :] = jnp.zeros((SUBLANES, LANES), F32)
        else:
            xprev_ref[j, :SUBLANES, :] = jnp.where(first, 0.0, xprev_ref[j, :SUBLANES, :])
        xprev_ref[j, SUBLANES:, :] = xj
        xcj = cb_ref[:, lanes] + cw_ref[LRU_CONV - 1:LRU_CONV, lanes] * xj
        for k in range(1, LRU_CONV):
            tap = xprev_ref[j, pl.ds(SUBLANES - k, ts, stride=1), :]
            xcj = xcj + cw_ref[LRU_CONV - 1 - k:LRU_CONV - k, lanes] * tap
        xprev_ref[j, :SUBLANES, :] = xj[ts - SUBLANES:]
        xcs.append(xcj)
    xc = jnp.concatenate(xcs, axis=1)
    xb = xc.astype(BF16)

    def gate_pre(w_ref, b_ref):
        parts = [_dot(xb[:, t0:t0 + MXU_TILE], w_ref[t0:t0 + MXU_TILE, t0:t0 + MXU_TILE])
                 for t0 in range(0, c, MXU_TILE)]
        return jnp.concatenate(parts, axis=1) + b_ref[...]

    pre_a = gate_pre(wa_ref, ba_ref)
    pre_x = gate_pre(wx_ref, bx_ref)
    z = -lam_ref[...]
    softplus = jnp.maximum(z, 0.0) + jnp.log1p(jnp.exp(-jnp.abs(z)))
    row = lax.broadcasted_iota(jnp.int32, (SUBLANES, LANES), 0)
    yield

    blk = SUBLANES * SUBLANES
    for r0 in range(0, ts, blk):
        rows = slice(r0, r0 + blk)
        for j in range(slabs):
            lanes = slice(j * LANES, (j + 1) * LANES)
            log_a = (-LRU_C) * _sigmoid(pre_a[rows, lanes]) * softplus[:, lanes]
            a_ref[j, rows, :] = jnp.exp(log_a)
            b_ref[j, rows, :] = (jnp.sqrt(-_expm1(2.0 * log_a))
                                 * (_sigmoid(pre_x[rows, lanes]) * xc[rows, lanes]))
            hs, ps = [], []
            for r in range(SUBLANES):
                a = a_ref[j, pl.ds(r0 + r, SUBLANES, stride=SUBLANES), :]
                b = b_ref[j, pl.ds(r0 + r, SUBLANES, stride=SUBLANES), :]
                hs.append(b if r == 0 else a * hs[-1] + b)
                ps.append(a if r == 0 else a * ps[-1])
            p, q = ps[-1], hs[-1]
            for k in (1, 2, 4):
                p_sh = jnp.where(row >= k, pltpu.roll(p, k, 0), 1.0)
                q_sh = jnp.where(row >= k, pltpu.roll(q, k, 0), 0.0)
                q = p * q_sh + q
                p = p * p_sh
            seg_out = p * h_in[j] + q
            seg_in = jnp.where(row == 0, h_in[j], pltpu.roll(seg_out, 1, 0))
            for r in range(SUBLANES):
                h_ref[j, pl.ds(r0 + r, SUBLANES, stride=SUBLANES), :] = ps[r] * seg_in + hs[r]
            h_in[j] = jnp.broadcast_to(seg_out[SUBLANES - 1:, :], (SUBLANES, LANES))
            y = h_ref[j, rows, :] * _gelu_x2(gate_ref[rows, lanes])
            y_ref[slot, rows, lanes] = y.astype(y_ref.dtype)
            yield
    hprev_ref[...] = jnp.concatenate(h_in, axis=1)


def _ffn_init(carry_ref):
    carry_ref[...] = jnp.zeros_like(carry_ref)


def _ffn(res_ref, next_starts_seq, g_ref, wg_ref, wu_ref, cw_ref, cb_ref, wd_ref, hn_ref, act_ref,
         carry_ref, gwork_ref, side_work=None, side_plan=None, starts_seq=None):
    tm = res_ref.shape[0]
    for r0 in range(0, tm, tm // 2):
        rows = slice(r0, r0 + tm // 2)
        hn_ref[rows, :] = _rms(res_ref[rows, :], g_ref[...]).astype(BF16)

    if starts_seq is not None:
        @pl.when(starts_seq)
        def _():
            _ffn_init(carry_ref)

    hn = hn_ref[...]
    for f in range(N_FF_CHUNKS):
        cols = slice(f * FF_CHUNK, (f + 1) * FF_CHUNK)
        g = _dot(hn, wg_ref[:, cols])
        u = _dot(hn, wu_ref[:, cols])
        ys = []
        for j in range(FF_CHUNK // LANES):
            slab = f * (FF_CHUNK // LANES) + j
            work = slab % gwork_ref.shape[0]
            lanes = slice(cols.start + j * LANES, cols.start + (j + 1) * LANES)
            gj = g[:, j * LANES:(j + 1) * LANES]
            gwork_ref[work, :SUBLANES, :] = carry_ref[slab]
            gwork_ref[work, SUBLANES:, :] = gj
            yj = cb_ref[:, lanes] + cw_ref[FFN_CONV - 1:FFN_CONV, lanes] * gj
            for k in range(1, FFN_CONV):
                tap = gwork_ref[work, pl.ds(SUBLANES - k, tm, stride=1), :]
                yj = yj + cw_ref[FFN_CONV - 1 - k:FFN_CONV - k, lanes] * tap
            carry_ref[slab] = jnp.where(next_starts_seq, 0.0, gj[tm - SUBLANES:])
            ys.append(yj)
        y = jnp.concatenate(ys, axis=1)
        act_ref[:, cols] = (_gelu_x2(y) * u).astype(BF16)
        if side_work is not None:
            for _ in range(side_plan[f]):
                next(side_work, None)
    for c0 in range(0, D_MODEL, DOWN_COLS):
        cols = slice(c0, c0 + DOWN_COLS)
        res_ref[:, cols] = res_ref[:, cols] + _dot(act_ref[...], wd_ref[:, cols])
        if side_work is not None:
            for _ in range(side_plan[N_FF_CHUNKS + c0 // DOWN_COLS]):
                next(side_work, None)
    if side_work is not None:
        for _ in side_work:
            pass


def _ffn_scratch(tm):
    return [pltpu.VMEM((tm, D_MODEL), BF16), pltpu.VMEM((tm, D_FF), BF16),
            pltpu.VMEM((D_FF // LANES, SUBLANES, LANES), F32),
            pltpu.VMEM((2 * FF_CHUNK // LANES, SUBLANES + tm, LANES), F32)]


def _ab_out_ffn_kernel(h_ref, ya_ref, x0_ref, gate0_ref, xn_ref, gaten_ref,
                       lcw_ref, lcb_ref, wa_ref, ba_ref, wx_ref, bx_ref, lam_ref,
                       woa_ref, wob_ref, g_ref, wg_ref, wu_ref, cw_ref, cb_ref, wd_ref, o_ref,
                       ylru_ref, xprev_ref, hprev_ref, sa_ref, sb_ref, sh_ref,
                       hn_ref, act_ref, carry_ref, gwork_ref, *, tiles_per_seq):
    i = pl.program_id(0)
    slot = i % 2
    lru = (lcw_ref, lcb_ref, wa_ref, ba_ref, wx_ref, bx_ref, lam_ref, xprev_ref, hprev_ref,
           sa_ref, sb_ref, sh_ref)

    @pl.when(i == 0)
    def _():
        _ffn_init(carry_ref)
        for _ in _rglru_tile(x0_ref, gate0_ref, True, lru, ylru_ref, 0):
            pass

    next_starts_seq = (i + 1) % tiles_per_seq == 0
    side = _rglru_tile(xn_ref, gaten_ref, next_starts_seq, lru, ylru_ref, 1 - slot)
    next(side)
    half = o_ref.shape[0] // 2
    for r0 in (0, half):
        rows = slice(r0, r0 + half)
        o_ref[rows, :] = (h_ref[rows, :] + _dot(ya_ref[rows, :], woa_ref[...])
                          + _dot(ylru_ref[slot, rows, :], wob_ref[...]))
    _ffn(o_ref, next_starts_seq, g_ref, wg_ref, wu_ref, cw_ref, cb_ref, wd_ref,
         hn_ref, act_ref, carry_ref, gwork_ref, side_work=side, side_plan=LRU_SIDE_PLAN,
         starts_seq=i % tiles_per_seq == 0)


def _ab_out_ffn(h, ya, x_lru, gate_lru, lru_consts, woa, wob, ffn, layer, seq_len):
    n = h.shape[0]
    tm = ROW_TILE
    n_tiles = n // tm
    row = lambda c: pl.BlockSpec((tm, c), lambda i: (i, 0))
    first = pl.BlockSpec((tm, LRU_WIDTH), lambda i: (0, 0), pipeline_mode=pl.Buffered(1))
    ahead = pl.BlockSpec((tm, LRU_WIDTH), lambda i: (jnp.minimum(i + 1, n_tiles - 1), 0))
    consts = (*lru_consts, woa, wob, *ffn)
    return pl.pallas_call(
        functools.partial(_ab_out_ffn_kernel, tiles_per_seq=seq_len // tm),
        grid=(n_tiles,),
        in_specs=[row(D_MODEL), row(ya.shape[1]), first, first, ahead, ahead]
                 + [_const_spec(a.shape) for a in (*lru_consts, woa, wob)]
                 + [_layer_spec(a.shape, layer) for a in ffn],
        out_specs=row(D_MODEL),
        out_shape=jax.ShapeDtypeStruct((n, D_MODEL), F32),
        scratch_shapes=[pltpu.VMEM((2, tm, LRU_WIDTH), BF16),
                        pltpu.VMEM((LRU_WIDTH // LANES, SUBLANES + tm, LANES), F32),
                        pltpu.VMEM((SUBLANES, LRU_WIDTH), F32)]
                       + [pltpu.VMEM((LRU_WIDTH // LANES, tm, LANES), F32)] * 3 + _ffn_scratch(tm),
        compiler_params=pltpu.CompilerParams(dimension_semantics=("arbitrary",),
                                             vmem_limit_bytes=VMEM_LIMIT),
        name="ab_out_ffn",
    )(h, ya, x_lru, gate_lru, x_lru, gate_lru, *consts)


def _sgu_ffn_kernel(h_ref, cg_ref, win_ref, lng_ref, lnb_ref, ws_ref, bs_ref, wout_ref,
                    g_ref, wg_ref, wu_ref, cw_ref, cb_ref, wd_ref, fg_ref, o_ref,
                    u_ref, v_ref, gated_ref, hn_ref, act_ref, carry_ref, gwork_ref, *, tiles_per_seq):
    i = pl.program_id(0)

    @pl.when(i == 0)
    def _():
        _ffn_init(carry_ref)

    tiles = h_ref.shape[0] // ROW_TILE
    for sub in range(tiles):
        rows = pl.ds(sub * ROW_TILE, ROW_TILE)
        next_starts_seq = (i * tiles + sub + 1) % tiles_per_seq == 0
        _sgu_ffn_tile(h_ref.at[rows], cg_ref, win_ref, lng_ref, lnb_ref, ws_ref, bs_ref, wout_ref,
                      g_ref, wg_ref, wu_ref, cw_ref, cb_ref, wd_ref, fg_ref, o_ref.at[rows],
                      u_ref, v_ref, gated_ref, hn_ref, act_ref, carry_ref, gwork_ref, next_starts_seq)


def _sgu_ffn_tile(h_ref, cg_ref, win_ref, lng_ref, lnb_ref, ws_ref, bs_ref, wout_ref,
                  g_ref, wg_ref, wu_ref, cw_ref, cb_ref, wd_ref, fg_ref, o_ref,
                  u_ref, v_ref, gated_ref, hn_ref, act_ref, carry_ref, gwork_ref, next_starts_seq):
    tm = h_ref.shape[0]
    h = h_ref[...]
    xn = _rms(h, cg_ref[...]).astype(BF16)
    v = _gelu_x2(_dot(xn, win_ref[:, D_MODEL:]))
    mu = jnp.mean(v, axis=-1, keepdims=True)
    vc = v - mu
    var = jnp.mean(vc * vc, axis=-1, keepdims=True)
    v_ref[...] = (vc * lax.rsqrt(var + 4.0 * NORM_EPS) * lng_ref[...] + lnb_ref[...]).astype(BF16)
    u_ref[...] = _gelu_x2(_dot(xn, win_ref[:, :D_MODEL]))

    n_chunks = tm // CHUNK
    r = lax.broadcasted_iota(jnp.int32, (CHUNK, CHUNK), 0)
    c = lax.broadcasted_iota(jnp.int32, (CHUNK, CHUNK), 1)
    for gp in range(SGU_GROUPS):
        lanes = slice(gp * CHUNK, (gp + 1) * CHUNK)
        w = jnp.where(c <= r, ws_ref[gp], 0.0).astype(BF16)
        rhs = jnp.concatenate([v_ref[ck * CHUNK:(ck + 1) * CHUNK, lanes] for ck in range(n_chunks)],
                              axis=1)
        sg = _dot(w, rhs) + bs_ref[:, gp:gp + 1]
        for ck in range(n_chunks):
            rows = slice(ck * CHUNK, (ck + 1) * CHUNK)
            gated_ref[rows, lanes] = (u_ref[rows, lanes] * sg[:, ck * CHUNK:(ck + 1) * CHUNK]).astype(BF16)

    o_ref[...] = h_ref[...] + _dot(gated_ref[...], wout_ref[...])
    _ffn(o_ref, next_starts_seq, g_ref, wg_ref, wu_ref, cw_ref, cb_ref, wd_ref, hn_ref, act_ref,
         carry_ref, gwork_ref)
    o_ref[...] = _rms(o_ref[...], fg_ref[...])


def _sgu_ffn(h, cg, win, lng, lnb, ws, bs_t, wout, ffn, layer, fg, seq_len):
    n = h.shape[0]
    tm = ROW_TILE
    row = lambda c: pl.BlockSpec((SGU_STEP_ROWS, c), lambda i: (i, 0))
    sgu = (cg, win, lng, lnb, ws, bs_t, wout)
    consts = (*sgu, *ffn, fg)
    return pl.pallas_call(
        functools.partial(_sgu_ffn_kernel, tiles_per_seq=seq_len // tm),
        grid=(n // SGU_STEP_ROWS,),
        in_specs=[row(D_MODEL)] + [_const_spec(a.shape) for a in sgu]
                 + [_layer_spec(a.shape, layer) for a in ffn] + [_const_spec(fg.shape)],
        out_specs=row(D_MODEL),
        out_shape=jax.ShapeDtypeStruct((n, D_MODEL), F32),
        scratch_shapes=[pltpu.VMEM((tm, D_MODEL), F32), pltpu.VMEM((tm, D_MODEL), BF16),
                        pltpu.VMEM((tm, D_MODEL), BF16)] + _ffn_scratch(tm),
        compiler_params=pltpu.CompilerParams(dimension_semantics=("arbitrary",),
                                             vmem_limit_bytes=VMEM_LIMIT),
        name="sgu_ffn",
    )(h, *consts)


def _ffn_params(norm, w_gate, w_up, conv_w, conv_b, w_down):
    depth = norm.shape[0]
    return (norm.reshape(depth, 1, D_MODEL), w_gate.astype(BF16), (0.5 * w_up).astype(BF16),
            conv_w, conv_b.reshape(depth, 1, D_FF), w_down.astype(BF16))


def _block_diag(w):
    heads, blk, _ = w.shape
    eye = jnp.eye(heads, dtype=w.dtype)
    return (w[:, :, None, :] * eye[:, None, :, None]).reshape(heads * blk, heads * blk)


def kernel(x, positions, ab_norm, ab_w_in, ab_q_norm, ab_w_q_b, ab_kv_norm, ab_w_kv_b, ab_conv_w, ab_conv_b, ab_w_rg_a, ab_b_rg_a, ab_w_rg_x, ab_b_rg_x, ab_lambda, ab_w_out, c_norm, c_w_in, c_ln_g, c_ln_b, c_w_s, c_b_s, c_w_out, ffn_norm, ffn_w_gate, ffn_w_up, ffn_conv_w, ffn_conv_b, ffn_w_down, final_norm):
    b, s, d = x.shape
    n = b * s
    h = x.reshape(n, d)
    pos = positions.reshape(n, 1)

    w_in = ab_w_in[0]
    o2 = Q_LORA + KV_LORA
    o3 = o2 + QK_ROPE
    zeros = lambda c: jnp.zeros((d, c), w_in.dtype)
    w_in_p = jnp.concatenate([w_in[:, :o2], zeros(QK_NOPE), w_in[:, o2:o3],
                              zeros(HEAD_PAD - QK_NOPE - QK_ROPE), w_in[:, o3:]], axis=1).astype(BF16)
    qk = QK_NOPE + QK_ROPE
    half = QK_ROPE // 2
    wq3 = ab_w_q_b[0].reshape(Q_LORA, MLA_HEADS, qk)
    pad_q = lambda w: jnp.pad(w, ((0, 0), (0, 0), (0, HEAD_PAD - w.shape[-1]))
                              ).reshape(Q_LORA, MLA_HEADS * HEAD_PAD)
    wq_rot = jnp.concatenate([jnp.zeros_like(wq3[..., :QK_NOPE]), -wq3[..., QK_NOPE + half:],
                              wq3[..., QK_NOPE:QK_NOPE + half]], axis=-1)
    wq = jnp.concatenate([pad_q(wq3), pad_q(wq_rot)], axis=1).astype(BF16)
    wkv = ab_w_kv_b[0].reshape(KV_LORA, MLA_HEADS, QK_NOPE + V_HEAD)
    wk = jnp.pad(wkv[:, :, :QK_NOPE], ((0, 0), (0, 0), (0, HEAD_PAD - QK_NOPE))
                 ).reshape(KV_LORA, MLA_HEADS * HEAD_PAD).astype(BF16)
    wv = wkv[:, :, QK_NOPE:].reshape(KV_LORA, MLA_HEADS * V_HEAD).astype(BF16)
    freq = jnp.exp(-math.log(ROPE_BASE) * jnp.arange(half, dtype=F32) / half)
    invf = jnp.tile(freq, LANES // half).reshape(1, LANES)

    q, k, v, x_lru, gate_lru = _ab_in(
        h, pos, ab_norm[0].reshape(1, d), w_in_p, ab_q_norm[0].reshape(1, Q_LORA), wq,
        ab_kv_norm[0].reshape(1, KV_LORA), wk, wv, invf)

    y_mla = _attention(q.reshape(b, s, -1), k.reshape(b, s, -1), v.reshape(b, s, -1))
    lru_consts = (ab_conv_w[0], ab_conv_b[0].reshape(1, -1),
                  _block_diag(ab_w_rg_a[0]).astype(BF16), ab_b_rg_a[0].reshape(1, -1),
                  _block_diag(ab_w_rg_x[0]).astype(BF16), ab_b_rg_x[0].reshape(1, -1),
                  ab_lambda[0].reshape(1, -1))

    w_out = ab_w_out[0].astype(BF16)
    mla_w = MLA_HEADS * V_HEAD
    ffn = _ffn_params(ffn_norm, ffn_w_gate, ffn_w_up, ffn_conv_w, ffn_conv_b, ffn_w_down)
    h = _ab_out_ffn(h, y_mla.reshape(n, mla_w), x_lru, gate_lru, lru_consts,
                    w_out[:mla_w], 0.5 * w_out[mla_w:], ffn, 0, seq_len=s)

    out = _sgu_ffn(h, c_norm[0].reshape(1, d), c_w_in[0].astype(BF16), c_ln_g[0].reshape(1, -1),
                   c_ln_b[0].reshape(1, -1), 0.5 * c_w_s[0], 0.5 * c_b_s[0].T, c_w_out[0].astype(BF16),
                   ffn, 1, final_norm.reshape(1, d), seq_len=s)
    return out.reshape(b, s, d)
```
